```python
import jax, jax.numpy as jnp
from jax import lax
import numpy as np

D_MODEL = 4096
BATCH = 2
SEQ = 8192
DEPTH = 2

N_EVEN = (DEPTH + 1) // 2
N_ODD = DEPTH // 2
D_CONV = D_MODEL // 2
D_POOL = D_MODEL // 2
N_POOL_GROUPS = 4
POOL_WINDOWS = (2, 4, 8, 16)
D_POOL_GROUP = D_POOL // N_POOL_GROUPS
CONV_A_WIDTH = 31
D_SHORT = D_MODEL
SHORT_CONV_WIDTH = 3
D_FF = ((8 * D_MODEL + 3 * 256 - 1) // (3 * 256)) * 256
N_MOD = 6
EPS = 1e-6

kernel_name = "hybrid_conv_pool_shortconv_adaln_trunk"


def rms_norm(x, g):
    xf = x.astype(jnp.float32)
    y = xf * lax.rsqrt(jnp.mean(xf * xf, axis=-1, keepdims=True) + EPS)
    return (y * g.astype(jnp.float32)).astype(x.dtype)


def layer_norm(x, g, b):
    xf = x.astype(jnp.float32)
    mu = jnp.mean(xf, axis=-1, keepdims=True)
    xc = xf - mu
    var = jnp.mean(xc * xc, axis=-1, keepdims=True)
    y = xc * lax.rsqrt(var + EPS) * g.astype(jnp.float32) + b.astype(jnp.float32)
    return y.astype(x.dtype)


def modulate(h, shift, scale):
    return h * (1 + scale[:, None, :]) + shift[:, None, :]


def causal_depthwise_conv(x, w):
    K, C = w.shape
    return lax.conv_general_dilated(
        x, w[:, None, :], window_strides=(1,), padding=[(K - 1, 0)],
        dimension_numbers=('NWC', 'WIO', 'NWC'), feature_group_count=C)


def conformer_conv_group(a_val, a_gate, conv_w, conv_b, ln_g, ln_b):
    a = a_val * jax.nn.sigmoid(a_gate)
    a = causal_depthwise_conv(a, conv_w) + conv_b
    return jax.nn.silu(layer_norm(a, ln_g, ln_b))


def multiscale_pool_group(v, pool_w, pool_scale):
    S = v.shape[1]
    vf = v.astype(jnp.float32)
    cs = jnp.pad(jnp.cumsum(vf, axis=1), ((0, 0), (1, 0), (0, 0)))
    t1 = jnp.arange(1, S + 1, dtype=jnp.float32)[:, None]
    outs = []
    for g, w in enumerate(POOL_WINDOWS):
        csg = cs[..., g * D_POOL_GROUP:(g + 1) * D_POOL_GROUP]
        hi = csg[:, 1:]
        lo = jnp.pad(csg[:, :S - w + 1], ((0, 0), (w - 1, 0), (0, 0)))
        mean = (hi - lo) / jnp.minimum(t1, float(w))
        outs.append(mean - vf[..., g * D_POOL_GROUP:(g + 1) * D_POOL_GROUP])
    pooled = jnp.stack(outs, axis=-2).astype(v.dtype)
    mixed = jnp.einsum('bsgc,gcd->bsgd', pooled, pool_w)
    return mixed.reshape(v.shape) * pool_scale


def swiglu_ffn(h, w1, w3, w2):
    return (jax.nn.silu(h @ w1) * (h @ w3)) @ w2


def setup_inputs(seed: int = 0) -> dict:
    key = jax.random.key(seed)
    ks = jax.random.split(key, 24)

    def nrm(k, shape, scale):
        return jax.random.normal(k, shape, jnp.float32) * scale

    D = D_MODEL
    return {
        "x": nrm(ks[0], (BATCH, SEQ, D), 1.0),
        "c": nrm(ks[1], (BATCH, D), 1.0),
        "mod_w": nrm(ks[2], (DEPTH, D, N_MOD * D), 0.5 * D ** -0.5),
        "mod_b": nrm(ks[3], (DEPTH, N_MOD * D), 0.02),
        "norm_mix_g": 1.0 + nrm(ks[4], (DEPTH, D), 0.02),
        "norm_ffn_g": 1.0 + nrm(ks[5], (DEPTH, D), 0.02),
        "ab_in_w": nrm(ks[6], (N_EVEN, D, 2 * D_CONV + D_POOL), D ** -0.5),
        "conv_a_w": nrm(ks[7], (N_EVEN, CONV_A_WIDTH, D_CONV), CONV_A_WIDTH ** -0.5),
        "conv_a_b": nrm(ks[8], (N_EVEN, D_CONV), 0.02),
        "ln_a_g": 1.0 + nrm(ks[9], (N_EVEN, D_CONV), 0.02),
        "ln_a_b": nrm(ks[10], (N_EVEN, D_CONV), 0.02),
        "pool_w": nrm(ks[11], (N_EVEN, N_POOL_GROUPS, D_POOL_GROUP, D_POOL_GROUP), D_POOL_GROUP ** -0.5),
        "pool_scale": 1.0 + nrm(ks[12], (N_EVEN, D_POOL), 0.1),
        "ab_out_w": nrm(ks[13], (N_EVEN, D_CONV + D_POOL, D), (D_CONV + D_POOL) ** -0.5),
        "c_in_w": nrm(ks[14], (N_ODD, D, 3 * D_SHORT), D ** -0.5),
        "conv_c_w": nrm(ks[15], (N_ODD, SHORT_CONV_WIDTH, D_SHORT), SHORT_CONV_WIDTH ** -0.5),
        "c_out_w": nrm(ks[16], (N_ODD, D_SHORT, D), D_SHORT ** -0.5),
        "ffn_w1": nrm(ks[17], (DEPTH, D, D_FF), D ** -0.5),
        "ffn_w3": nrm(ks[18], (DEPTH, D, D_FF), D ** -0.5),
        "ffn_w2": nrm(ks[19], (DEPTH, D_FF, D), D_FF ** -0.5),
        "final_g": 1.0 + nrm(ks[20], (D,), 0.02),
    }


def reference(x, c, mod_w, mod_b, norm_mix_g, norm_ffn_g, ab_in_w, conv_a_w, conv_a_b,
              ln_a_g, ln_a_b, pool_w, pool_scale, ab_out_w, c_in_w, conv_c_w, c_out_w,
              ffn_w1, ffn_w3, ffn_w2, final_g):
    c_act = jax.nn.silu(c)
    for i in range(DEPTH):
        mod = c_act @ mod_w[i] + mod_b[i]
        sh_m, sc_m, g_m, sh_f, sc_f, g_f = jnp.split(mod, N_MOD, axis=-1)

        h = modulate(rms_norm(x, norm_mix_g[i]), sh_m, sc_m)
        j = i // 2
        if i % 2 == 0:
            proj = h @ ab_in_w[j]
            a_val, a_gate, b_in = jnp.split(proj, [D_CONV, 2 * D_CONV], axis=-1)
            a_out = conformer_conv_group(a_val, a_gate, conv_a_w[j], conv_a_b[j], ln_a_g[j], ln_a_b[j])
            b_out = multiscale_pool_group(b_in, pool_w[j], pool_scale[j])
            y = jnp.concatenate([a_out, b_out], axis=-1) @ ab_out_w[j]
        else:
            proj = h @ c_in_w[j]
            gate_b, gate_c, v = jnp.split(proj, 3, axis=-1)
            y = (gate_b * causal_depthwise_conv(gate_c * v, conv_c_w[j])) @ c_out_w[j]
        x = x + g_m[:, None, :] * y

        h = modulate(rms_norm(x, norm_ffn_g[i]), sh_f, sc_f)
        x = x + g_f[:, None, :] * swiglu_ffn(h, ffn_w1[i], ffn_w3[i], ffn_w2[i])
    return rms_norm(x, final_g)
```

```python
import functools

import jax
import jax.numpy as jnp
from jax import lax
from jax.experimental import pallas as pl
from jax.experimental.pallas import tpu as pltpu

EPS = 1e-6
POOL_WINDOWS = (2, 4, 8, 16)
N_MOD = 6

F32 = jnp.float32
BF16 = jnp.bfloat16

V7X_VMEM_BYTES = 64 * 1024 * 1024
V7X_LANES = 128
V7X_SUBLANES = 8
VMEM_LIMIT_BYTES = V7X_VMEM_BYTES - 6 * 1024 * 1024
HALO_ROWS = 32


def _params(*semantics):
    return pltpu.CompilerParams(dimension_semantics=semantics, vmem_limit_bytes=VMEM_LIMIT_BYTES)


def _tile(n, want):
    t = min(n, want)
    while n % t:
        t //= 2
    return t


def _silu(v):
    return v * jax.nn.sigmoid(v)


def _mod_kernel(c_ref, w_ref, b_ref, o_ref):
    ca = _silu(c_ref[...]).astype(BF16)
    w = w_ref[0].astype(BF16)
    o_ref[0] = jnp.dot(ca, w, preferred_element_type=F32) + b_ref[0]


def _modulation(c, mod_w, mod_b):
    depth, d, n = mod_w.shape
    b = c.shape[0]
    rows = V7X_SUBLANES
    c_pad = jnp.zeros((rows, d), F32).at[:b].set(c)
    tn = _tile(n, 512)
    out = pl.pallas_call(
        _mod_kernel,
        grid=(depth, n // tn),
        in_specs=[
            pl.BlockSpec((rows, d), lambda i, j: (0, 0)),
            pl.BlockSpec((1, d, tn), lambda i, j: (i, 0, j)),
            pl.BlockSpec((1, 1, tn), lambda i, j: (i, 0, j)),
        ],
        out_specs=pl.BlockSpec((1, rows, tn), lambda i, j: (i, 0, j)),
        out_shape=jax.ShapeDtypeStruct((depth, rows, n), F32),
        compiler_params=_params("arbitrary", "arbitrary"),
        name="modulation",
    )(c_pad, mod_w, mod_b.reshape(depth, 1, n))
    return out[:, :b, :]


def _rms(x, g):
    ms = jnp.mean(x * x, axis=-1, keepdims=True)
    return x * lax.rsqrt(ms + EPS) * g


def _normmod_kernel(x_ref, g_ref, sh_ref, sc_ref, h_ref):
    h = _rms(x_ref[0], g_ref[...]) * (1.0 + sc_ref[0]) + sh_ref[0]
    h_ref[0] = h.astype(h_ref.dtype)


def _res_normmod_kernel(x_ref, y_ref, gate_ref, g_ref, sh_ref, sc_ref, xo_ref, h_ref):
    x = x_ref[0] + gate_ref[0] * y_ref[0]
    xo_ref[0] = x
    h = _rms(x, g_ref[...]) * (1.0 + sc_ref[0]) + sh_ref[0]
    h_ref[0] = h.astype(h_ref.dtype)


def _res_norm_kernel(x_ref, y_ref, gate_ref, g_ref, o_ref):
    x = x_ref[0] + gate_ref[0] * y_ref[0]
    o_ref[0] = _rms(x, g_ref[...])


def _row_specs(b, s, d, ts):
    tile = pl.BlockSpec((1, ts, d), lambda i, j: (i, j, 0))
    per_batch = pl.BlockSpec((1, 1, d), lambda i, j: (i, 0, 0))
    shared = pl.BlockSpec((1, d), lambda i, j: (0, 0))
    return tile, per_batch, shared


def _normmod(x, g, shift, scale):
    b, s, d = x.shape
    ts = _tile(s, 256)
    tile, per_batch, shared = _row_specs(b, s, d, ts)
    return pl.pallas_call(
        _normmod_kernel,
        grid=(b, s // ts),
        in_specs=[tile, shared, per_batch, per_batch],
        out_specs=tile,
        out_shape=jax.ShapeDtypeStruct((b, s, d), BF16),
        compiler_params=_params("parallel", "parallel"),
        name="normmod",
    )(x, g.reshape(1, d), shift.reshape(b, 1, d), scale.reshape(b, 1, d))


def _res_normmod(x, y, gate, g, shift, scale):
    b, s, d = x.shape
    ts = _tile(s, 256)
    tile, per_batch, shared = _row_specs(b, s, d, ts)
    return pl.pallas_call(
        _res_normmod_kernel,
        grid=(b, s // ts),
        in_specs=[tile, tile, per_batch, shared, per_batch, per_batch],
        out_specs=[tile, tile],
        out_shape=[jax.ShapeDtypeStruct((b, s, d), F32), jax.ShapeDtypeStruct((b, s, d), BF16)],
        compiler_params=_params("parallel", "parallel"),
        name="res_normmod",
    )(x, y, gate.reshape(b, 1, d), g.reshape(1, d), shift.reshape(b, 1, d), scale.reshape(b, 1, d))


def _res_norm(x, y, gate, g):
    b, s, d = x.shape
    ts = _tile(s, 256)
    tile, per_batch, shared = _row_specs(b, s, d, ts)
    return pl.pallas_call(
        _res_norm_kernel,
        grid=(b, s // ts),
        in_specs=[tile, tile, per_batch, shared],
        out_specs=tile,
        out_shape=jax.ShapeDtypeStruct((b, s, d), F32),
        compiler_params=_params("parallel", "parallel"),
        name="res_norm",
    )(x, y, gate.reshape(b, 1, d), g.reshape(1, d))


def _mm_kernel(x_ref, w_ref, o_ref):
    o_ref[...] = jnp.dot(x_ref[...], w_ref[...], preferred_element_type=F32).astype(o_ref.dtype)


def _matmul(x, w, out_dtype):
    m, k = x.shape
    n = w.shape[1]
    bm = _tile(m, 1024)
    bn = _tile(n, 1024)
    return pl.pallas_call(
        _mm_kernel,
        grid=(m // bm, n // bn),
        in_specs=[pl.BlockSpec((bm, k), lambda i, j: (i, 0)),
                  pl.BlockSpec((k, bn), lambda i, j: (0, j))],
        out_specs=pl.BlockSpec((bm, bn), lambda i, j: (i, j)),
        out_shape=jax.ShapeDtypeStruct((m, n), out_dtype),
        compiler_params=_params("parallel", "arbitrary"),
        name="matmul",
    )(x, w)


def _mixer_ab_kernel(av_ref, ag_ref, bv_ref, hav_ref, hag_ref, hbv_ref,
                     cw_ref, cb_ref, lg_ref, lb_ref, pw_ref, ps_ref,
                     o_ref, abuf, shifted, ybuf, vbuf):
    s = pl.program_id(1)
    ts = av_ref.shape[1]
    dc = av_ref.shape[2]
    dp = bv_ref.shape[2]
    taps = cw_ref.shape[0]
    n_groups = pw_ref.shape[0]
    dg = dp // n_groups
    first = s == 0

    a_halo = hav_ref[0] * jax.nn.sigmoid(hag_ref[0])
    abuf[0:HALO_ROWS, :] = jnp.where(first, 0.0, a_halo)
    abuf[HALO_ROWS:, :] = av_ref[0] * jax.nn.sigmoid(ag_ref[0])
    vbuf[0:HALO_ROWS, :] = jnp.where(first, 0.0, hbv_ref[0])
    vbuf[HALO_ROWS:, :] = bv_ref[0]

    base = HALO_ROWS - (taps - 1)
    rows = V7X_SUBLANES
    n_shift = shifted.shape[1]
    for r in range(1, rows):
        shifted[r - 1] = abuf[pl.ds(r, n_shift), :]

    def tap(k, r0, lanes):
        q, r = divmod(base + k, rows)
        start = pl.multiple_of(r0 + q * rows, rows)
        if r == 0:
            return abuf[pl.ds(start, rows), lanes]
        return shifted[r - 1, pl.ds(start, rows), lanes]

    for c in range(dc // V7X_LANES):
        lanes = pl.ds(c * V7X_LANES, V7X_LANES)
        wk = [jnp.broadcast_to(cw_ref[k:k + 1, lanes], (rows, V7X_LANES)) for k in range(taps)]
        bias = jnp.broadcast_to(cb_ref[:, lanes], (rows, V7X_LANES))

        def row_body(i, carry, lanes=lanes, wk=wk, bias=bias):
            r0 = pl.multiple_of(i * rows, rows)
            acc = bias
            for k in range(taps):
                acc = acc + wk[k] * tap(k, r0, lanes)
            ybuf[pl.ds(r0, rows), lanes] = acc
            return carry

        lax.fori_loop(0, ts // rows, row_body, 0, unroll=4)

    y = ybuf[...]
    mu = jnp.mean(y, axis=-1, keepdims=True)
    yc = y - mu
    var = jnp.mean(yc * yc, axis=-1, keepdims=True)
    a_out = _silu(yc * lax.rsqrt(var + EPS) * lg_ref[...] + lb_ref[...])
    o_ref[0, :, 0:dc] = a_out.astype(o_ref.dtype)

    t1 = (s * ts + 1 + lax.broadcasted_iota(jnp.int32, (ts, dg), 0)).astype(F32)
    for g, w in enumerate(POOL_WINDOWS[:n_groups]):
        lanes = pl.ds(g * dg, dg)
        v = vbuf[pl.ds(HALO_ROWS, ts), lanes]
        acc = v
        for j in range(1, w):
            acc = acc + vbuf[pl.ds(HALO_ROWS - j, ts), lanes]
        pooled = acc / jnp.minimum(t1, float(w)) - v
        mixed = jnp.dot(pooled.astype(BF16), pw_ref[g], preferred_element_type=F32)
        o_ref[0, :, pl.ds(dc + g * dg, dg)] = (mixed * ps_ref[:, lanes]).astype(o_ref.dtype)


def _mixer_ab(proj, conv_w, conv_b, ln_g, ln_b, pool_w, pool_scale):
    b, s, _ = proj.shape
    taps, dc = conv_w.shape
    dp = pool_scale.shape[0]
    assert dc == dp and proj.shape[2] == 2 * dc + dp
    assert taps - 1 <= HALO_ROWS and max(POOL_WINDOWS) - 1 <= HALO_ROWS
    ts = _tile(s, 256)
    halo_per_tile = ts // HALO_ROWS

    def cur(col):
        return pl.BlockSpec((1, ts, dc), lambda i, j: (i, j, col))

    def halo(col):
        return pl.BlockSpec((1, HALO_ROWS, dc),
                            lambda i, j: (i, jnp.maximum(j * halo_per_tile - 1, 0), col))

    def whole(shape):
        return pl.BlockSpec(shape, lambda i, j: (0,) * len(shape))

    return pl.pallas_call(
        _mixer_ab_kernel,
        grid=(b, s // ts),
        in_specs=[cur(0), cur(1), cur(2), halo(0), halo(1), halo(2),
                  whole((taps, dc)), whole((1, dc)), whole((1, dc)), whole((1, dc)),
                  whole(pool_w.shape), whole((1, dp))],
        out_specs=pl.BlockSpec((1, ts, dc + dp), lambda i, j: (i, j, 0)),
        out_shape=jax.ShapeDtypeStruct((b, s, dc + dp), BF16),
        scratch_shapes=[pltpu.VMEM((HALO_ROWS + ts, dc), F32),
                        pltpu.VMEM((V7X_SUBLANES - 1, HALO_ROWS + ts - V7X_SUBLANES, dc), F32),
                        pltpu.VMEM((ts, dc), F32),
                        pltpu.VMEM((HALO_ROWS + ts, dp), F32)],
        compiler_params=_params("parallel", "parallel"),
        name="mixer_ab",
    )(proj, proj, proj, proj, proj, proj,
      conv_w, conv_b.reshape(1, dc), ln_g.reshape(1, dc), ln_b.reshape(1, dc),
      pool_w.astype(BF16), pool_scale.reshape(1, dp))


def _gate_proj_kernel(x_ref, wb_ref, wc_ref, wv_ref, gb_ref, z_ref):
    x = x_ref[...]
    gb = jnp.dot(x, wb_ref[...], preferred_element_type=F32)
    gc = jnp.dot(x, wc_ref[...], preferred_element_type=F32)
    v = jnp.dot(x, wv_ref[...], preferred_element_type=F32)
    gb_ref[...] = gb.astype(gb_ref.dtype)
    z_ref[...] = (gc * v).astype(z_ref.dtype)


def _gate_proj(h, w):
    m, k = h.shape
    ds = w.shape[1] // 3
    bm = _tile(m, 1024)
    bn = _tile(ds, 512)
    nb = ds // bn

    def wspec(group):
        return pl.BlockSpec((k, bn), lambda i, j: (0, j + group * nb))

    out = pl.BlockSpec((bm, bn), lambda i, j: (i, j))
    return pl.pallas_call(
        _gate_proj_kernel,
        grid=(m // bm, nb),
        in_specs=[pl.BlockSpec((bm, k), lambda i, j: (i, 0)), wspec(0), wspec(1), wspec(2)],
        out_specs=[out, out],
        out_shape=[jax.ShapeDtypeStruct((m, ds), BF16), jax.ShapeDtypeStruct((m, ds), BF16)],
        compiler_params=_params("parallel", "arbitrary"),
        name="gate_proj",
    )(h, w, w, w)


SHORT_HALO_ROWS = 16


def _gate_conv_kernel(gb_ref, z_ref, hz_ref, cw_ref, u_ref, zbuf):
    s = pl.program_id(1)
    ts = z_ref.shape[1]
    taps = cw_ref.shape[0]
    zbuf[0:SHORT_HALO_ROWS, :] = jnp.where(s == 0, 0.0, hz_ref[0].astype(F32))
    zbuf[SHORT_HALO_ROWS:, :] = z_ref[0].astype(F32)
    base = SHORT_HALO_ROWS - (taps - 1)
    conv = cw_ref[0:1, :] * zbuf[pl.ds(base, ts), :]
    for k in range(1, taps):
        conv = conv + cw_ref[k:k + 1, :] * zbuf[pl.ds(base + k, ts), :]
    u_ref[0] = (gb_ref[0].astype(F32) * conv).astype(u_ref.dtype)


def _gate_conv(gb, z, conv_w):
    b, s, ds = z.shape
    taps = conv_w.shape[0]
    assert taps - 1 <= SHORT_HALO_ROWS
    ts = _tile(s, 256)
    halo_per_tile = ts // SHORT_HALO_ROWS
    tile = pl.BlockSpec((1, ts, ds), lambda i, j: (i, j, 0))
    return pl.pallas_call(
        _gate_conv_kernel,
        grid=(b, s // ts),
        in_specs=[tile, tile,
                  pl.BlockSpec((1, SHORT_HALO_ROWS, ds),
                               lambda i, j: (i, jnp.maximum(j * halo_per_tile - 1, 0), 0)),
                  pl.BlockSpec((taps, ds), lambda i, j: (0, 0))],
        out_specs=tile,
        out_shape=jax.ShapeDtypeStruct((b, s, ds), BF16),
        scratch_shapes=[pltpu.VMEM((SHORT_HALO_ROWS + ts, ds), F32)],
        compiler_params=_params("parallel", "parallel"),
        name="gate_conv",
    )(gb, z, z, conv_w)


def _ffn_kernel(h_ref, w1_ref, w3_ref, w2_ref, o_ref):
    @pl.when(pl.program_id(1) == 0)
    def _():
        o_ref[...] = jnp.zeros_like(o_ref)

    h = h_ref[...]
    a = jnp.dot(h, w1_ref[...], preferred_element_type=F32)
    b = jnp.dot(h, w3_ref[...], preferred_element_type=F32)
    u = (_silu(a) * b).astype(BF16)
    o_ref[...] += jnp.dot(u, w2_ref[...], preferred_element_type=F32)


def _ffn(h, w1, w3, w2):
    m, d = h.shape
    dff = w1.shape[1]
    bm = _tile(m, 512)
    bf = _tile(dff, 256)
    return pl.pallas_call(
        _ffn_kernel,
        grid=(m // bm, dff // bf),
        in_specs=[pl.BlockSpec((bm, d), lambda i, j: (i, 0)),
                  pl.BlockSpec((d, bf), lambda i, j: (0, j)),
                  pl.BlockSpec((d, bf), lambda i, j: (0, j)),
                  pl.BlockSpec((bf, d), lambda i, j: (j, 0))],
        out_specs=pl.BlockSpec((bm, d), lambda i, j: (i, 0)),
        out_shape=jax.ShapeDtypeStruct((m, d), F32),
        compiler_params=_params("parallel", "arbitrary"),
        name="ffn",
    )(h, w1, w3, w2)


def kernel(x, c, mod_w, mod_b, norm_mix_g, norm_ffn_g, ab_in_w, conv_a_w, conv_a_b, ln_a_g, ln_a_b,
           pool_w, pool_scale, ab_out_w, c_in_w, conv_c_w, c_out_w, ffn_w1, ffn_w3, ffn_w2, final_g):
    b, s, d = x.shape
    depth = mod_w.shape[0]
    m = b * s
    mod = _modulation(c, mod_w, mod_b)

    def mod_part(i, part):
        return mod[i, :, part * d:(part + 1) * d]

    h = _normmod(x, norm_mix_g[0], mod_part(0, 0), mod_part(0, 1))
    out = None
    for i in range(depth):
        j = i // 2
        h2 = h.reshape(m, d)
        if i % 2 == 0:
            proj = _matmul(h2, ab_in_w[j].astype(BF16), F32).reshape(b, s, -1)
            cat = _mixer_ab(proj, conv_a_w[j], conv_a_b[j], ln_a_g[j], ln_a_b[j], pool_w[j], pool_scale[j])
            y = _matmul(cat.reshape(m, -1), ab_out_w[j].astype(BF16), F32)
        else:
            gb, z = _gate_proj(h2, c_in_w[j].astype(BF16))
            u = _gate_conv(gb.reshape(b, s, -1), z.reshape(b, s, -1), conv_c_w[j])
            y = _matmul(u.reshape(m, -1), c_out_w[j].astype(BF16), F32)
        x, h = _res_normmod(x, y.reshape(b, s, d), mod_part(i, 2), norm_ffn_g[i], mod_part(i, 3), mod_part(i, 4))

        y = _ffn(h.reshape(m, d), ffn_w1[i].astype(BF16), ffn_w3[i].astype(BF16), ffn_w2[i].astype(BF16))
        y = y.reshape(b, s, d)
        if i + 1 < depth:
            x, h = _res_normmod(x, y, mod_part(i, 5), norm_mix_g[i + 1], mod_part(i + 1, 0), mod_part(i + 1, 1))
        else:
            out = _res_norm(x, y, mod_part(i, 5), final_g)
    return out
```

```python
import functools

import jax
import jax.numpy as jnp
from jax import lax
from jax.experimental import pallas as pl
from jax.experimental.pallas import tpu as pltpu

EPS = 1e-6
POOL_WINDOWS = (2, 4, 8, 16)

F32 = jnp.float32
BF16 = jnp.bfloat16

V7X_VMEM_BYTES = 64 * 1024 * 1024
V7X_LANES = 128
V7X_SUBLANES = 8
V7X_BF16_SUBLANES = 16
VMEM_LIMIT_BYTES = V7X_VMEM_BYTES - 6 * 1024 * 1024

ROW_TILE = 1024
NORM_CHUNK = 64
HALO_ROWS = 32
SHORT_HALO_ROWS = V7X_BF16_SUBLANES
CONV_PARTIAL_SUMS = 4

ONCE = pl.Buffered(1)


def _params(*semantics):
    return pltpu.CompilerParams(dimension_semantics=semantics, vmem_limit_bytes=VMEM_LIMIT_BYTES)


def _tile(n, want):
    t = min(n, want)
    while n % t:
        t //= 2
    return t


def _silu(v):
    return v * jax.nn.sigmoid(v)


def _rms(x, g):
    ms = jnp.mean(x * x, axis=-1, keepdims=True)
    return x * lax.rsqrt(ms + EPS) * g


def _mod_kernel(c_ref, w_ref, b_ref, o_ref):
    ca = _silu(c_ref[...]).astype(BF16)
    w = w_ref[0].astype(BF16)
    o_ref[0] = jnp.dot(ca, w, preferred_element_type=F32) + b_ref[0]


def _modulation(c, mod_w, mod_b):
    depth, d, n = mod_w.shape
    b = c.shape[0]
    rows = V7X_SUBLANES
    c_pad = jnp.zeros((rows, d), F32).at[:b].set(c)
    tn = _tile(n, 512)
    out = pl.pallas_call(
        _mod_kernel,
        grid=(depth, n // tn),
        in_specs=[
            pl.BlockSpec((rows, d), lambda i, j: (0, 0)),
            pl.BlockSpec((1, d, tn), lambda i, j: (i, 0, j)),
            pl.BlockSpec((1, 1, tn), lambda i, j: (i, 0, j)),
        ],
        out_specs=pl.BlockSpec((1, rows, tn), lambda i, j: (i, 0, j)),
        out_shape=jax.ShapeDtypeStruct((depth, rows, n), F32),
        compiler_params=_params("arbitrary", "arbitrary"),
        name="modulation",
    )(c_pad, mod_w, mod_b.reshape(depth, 1, n))
    return out[:, :b, :]


def _normmod_rows(x_ref, g_ref, sh_ref, sc_ref, h_scr):
    w = g_ref[...] * (1.0 + sc_ref[0])
    shift = sh_ref[0]
    chunk = _tile(x_ref.shape[0], NORM_CHUNK)

    def body(i, carry):
        rows = pl.ds(pl.multiple_of(i * chunk, chunk), chunk)
        h_scr[rows, :] = (_rms(x_ref[rows, :], w) + shift).astype(h_scr.dtype)
        return carry

    lax.fori_loop(0, x_ref.shape[0] // chunk, body, 0)


def _per_batch_spec(d, tiles_per_batch):
    return pl.BlockSpec((1, 1, d), lambda i, j: (i // tiles_per_batch, 0, 0))


def _proj3_kernel(x_ref, g_ref, sh_ref, sc_ref, w0_ref, w1_ref, w2_ref, o0_ref, o1_ref, h_scr, *, glu):
    @pl.when(pl.program_id(1) == 0)
    def _():
        _normmod_rows(x_ref, g_ref, sh_ref, sc_ref, h_scr)

    h = h_scr[...]
    p0 = jnp.dot(h, w0_ref[...], preferred_element_type=F32)
    p1 = jnp.dot(h, w1_ref[...], preferred_element_type=F32)
    p2 = jnp.dot(h, w2_ref[...], preferred_element_type=F32)
    if glu:
        o0_ref[...] = (p0 * jax.nn.sigmoid(p1)).astype(o0_ref.dtype)
        o1_ref[...] = p2.astype(o1_ref.dtype)
    else:
        o0_ref[...] = p0.astype(o0_ref.dtype)
        o1_ref[...] = (p1 * p2).astype(o1_ref.dtype)


def _proj3(x, g, shift, scale, w, *, tiles_per_batch, glu, out_dtype):
    m, d = x.shape
    n = w.shape[1] // 3
    bm = m // (shift.shape[0] * tiles_per_batch)
    bn = _tile(n, 256)
    nb = n // bn

    def wspec(group):
        return pl.BlockSpec((d, bn), lambda i, j: (0, j + group * nb))

    per_batch = _per_batch_spec(d, tiles_per_batch)
    out = pl.BlockSpec((bm, bn), lambda i, j: (i, j))
    return pl.pallas_call(
        functools.partial(_proj3_kernel, glu=glu),
        grid=(m // bm, nb),
        in_specs=[pl.BlockSpec((bm, d), lambda i, j: (i, 0), pipeline_mode=ONCE),
                  pl.BlockSpec((1, d), lambda i, j: (0, 0)), per_batch, per_batch,
                  wspec(0), wspec(1), wspec(2)],
        out_specs=[out, out],
        out_shape=[jax.ShapeDtypeStruct((m, n), out_dtype)] * 2,
        scratch_shapes=[pltpu.VMEM((bm, d), BF16)],
        compiler_params=_params("parallel", "arbitrary"),
        name="proj3_glu" if glu else "proj3_gate",
    )(x, g.reshape(1, d), shift, scale, w, w, w)


def _mm_res_kernel(u_ref, w_ref, x_ref, gate_ref, o_ref):
    y = jnp.dot(u_ref[...], w_ref[...], preferred_element_type=F32)
    o_ref[...] = x_ref[...] + gate_ref[0] * y


def _matmul_residual(u, w, x, gate, *, tiles_per_batch):
    m, k = u.shape
    n = w.shape[1]
    bm = m // (gate.shape[0] * tiles_per_batch)
    bn = _tile(n, 1024)
    tile = pl.BlockSpec((bm, bn), lambda i, j: (i, j))
    return pl.pallas_call(
        _mm_res_kernel,
        grid=(m // bm, n // bn),
        in_specs=[pl.BlockSpec((bm, k), lambda i, j: (i, 0)),
                  pl.BlockSpec((k, bn), lambda i, j: (0, j)),
                  tile,
                  pl.BlockSpec((1, 1, bn), lambda i, j: (i // tiles_per_batch, 0, j))],
        out_specs=tile,
        out_shape=jax.ShapeDtypeStruct((m, n), F32),
        compiler_params=_params("parallel", "arbitrary"),
        name="matmul_residual",
    )(u, w, x, gate)


def _mixer_ab_kernel(a_ref, v_ref, ha_ref, hv_ref, cw_ref, cb_ref, lg_ref, lb_ref, pw_ref, ps_ref,
                     o_ref, abuf, shifted, ybuf, vbuf, pbuf):
    s = pl.program_id(1)
    ts = a_ref.shape[1]
    dc = a_ref.shape[2]
    dp = v_ref.shape[2]
    taps = cw_ref.shape[0]
    n_groups = pw_ref.shape[0]
    dg = dp // n_groups
    first = s == 0

    abuf[0:HALO_ROWS, 0:dc] = jnp.where(first, 0.0, ha_ref[0])
    abuf[HALO_ROWS:, 0:dc] = a_ref[0]
    vbuf[0:HALO_ROWS, :] = jnp.where(first, 0.0, hv_ref[0])
    vbuf[HALO_ROWS:, :] = v_ref[0]

    base = HALO_ROWS - (taps - 1)
    rows = V7X_SUBLANES
    n_shift = shifted.shape[1]
    for r in range(1, rows):
        shifted[r - 1, :, 0:dc] = abuf[pl.ds(r, n_shift), 0:dc]

    def tap(k, r0, lanes):
        q, r = divmod(base + k, rows)
        start = pl.multiple_of(r0 + q * rows, rows)
        if r == 0:
            return abuf[pl.ds(start, rows), lanes]
        return shifted[r - 1, pl.ds(start, rows), lanes]

    for c in range(dc // V7X_LANES):
        lanes = pl.ds(c * V7X_LANES, V7X_LANES)
        wk = [jnp.broadcast_to(cw_ref[k:k + 1, lanes], (rows, V7X_LANES)) for k in range(taps)]
        bias = jnp.broadcast_to(cb_ref[:, lanes], (rows, V7X_LANES))

        def row_body(i, carry, lanes=lanes, wk=wk, bias=bias):
            r0 = pl.multiple_of(i * rows, rows)
            parts = [wk[k] * tap(k, r0, lanes) for k in range(CONV_PARTIAL_SUMS)]
            for k in range(CONV_PARTIAL_SUMS, taps):
                parts[k % CONV_PARTIAL_SUMS] = parts[k % CONV_PARTIAL_SUMS] + wk[k] * tap(k, r0, lanes)
            while len(parts) > 1:
                parts = [p + q for p, q in zip(parts[0::2], parts[1::2])] + parts[len(parts) & ~1:]
            ybuf[pl.ds(r0, rows), lanes] = parts[0] + bias
            return carry

        lax.fori_loop(0, ts // rows, row_body, 0, unroll=4)

    y = ybuf[...]
    mu = jnp.mean(y, axis=-1, keepdims=True)
    yc = y - mu
    var = jnp.mean(yc * yc, axis=-1, keepdims=True)
    a_out = _silu(yc * lax.rsqrt(var + EPS) * lg_ref[...] + lb_ref[...])
    o_ref[0, :, 0:dc] = a_out.astype(o_ref.dtype)

    t1 = (s * ts + 1 + lax.broadcasted_iota(jnp.int32, (ts, dg), 0)).astype(F32)
    for g, w in enumerate(POOL_WINDOWS[:n_groups]):
        lanes = pl.ds(g * dg, dg)
        v = vbuf[pl.ds(HALO_ROWS, ts), lanes]
        src, src_lanes, d, stage = vbuf, lanes, 1, 0
        while 2 * d < w:
            lo = (stage + 1) * V7X_SUBLANES
            n = HALO_ROWS + ts - lo
            dst = pbuf.at[stage % 2]
            dst[pl.ds(lo, n), :] = src[pl.ds(lo, n), src_lanes] + src[pl.ds(lo - d, n), src_lanes]
            src, src_lanes, d, stage = dst, slice(None), 2 * d, stage + 1
        acc = src[pl.ds(HALO_ROWS, ts), src_lanes] + src[pl.ds(HALO_ROWS - d, ts), src_lanes]
        pooled = acc / jnp.minimum(t1, float(w)) - v
        mixed = jnp.dot(pooled.astype(BF16), pw_ref[g], preferred_element_type=F32)
        o_ref[0, :, pl.ds(dc + g * dg, dg)] = (mixed * ps_ref[:, lanes]).astype(o_ref.dtype)


def _mixer_ab(a, v, conv_w, conv_b, ln_g, ln_b, pool_w, pool_scale):
    b, s, dc = a.shape
    dp = v.shape[2]
    taps = conv_w.shape[0]
    assert taps - 1 <= HALO_ROWS
    assert all(w & (w - 1) == 0 for w in POOL_WINDOWS) and 2 * max(POOL_WINDOWS) <= HALO_ROWS
    ts = _tile(s, 256)
    halo_per_tile = ts // HALO_ROWS

    def cur(width):
        return pl.BlockSpec((1, ts, width), lambda i, j: (i, j, 0))

    def halo(width):
        return pl.BlockSpec((1, HALO_ROWS, width),
                            lambda i, j: (i, jnp.maximum(j * halo_per_tile - 1, 0), 0))

    def whole(shape):
        return pl.BlockSpec(shape, lambda i, j: (0,) * len(shape))

    return pl.pallas_call(
        _mixer_ab_kernel,
        grid=(b, s // ts),
        in_specs=[cur(dc), cur(dp), halo(dc), halo(dp),
                  whole((taps, dc)), whole((1, dc)), whole((1, dc)), whole((1, dc)),
                  whole(pool_w.shape), whole((1, dp))],
        out_specs=pl.BlockSpec((1, ts, dc + dp), lambda i, j: (i, j, 0)),
        out_shape=jax.ShapeDtypeStruct((b, s, dc + dp), BF16),
        scratch_shapes=[pltpu.VMEM((HALO_ROWS + ts, dc + V7X_LANES), F32),
                        pltpu.VMEM((V7X_SUBLANES - 1, HALO_ROWS + ts - V7X_SUBLANES, dc + V7X_LANES), F32),
                        pltpu.VMEM((ts, dc), F32),
                        pltpu.VMEM((HALO_ROWS + ts, dp), F32),
                        pltpu.VMEM((2, HALO_ROWS + ts, dp // pool_w.shape[0]), F32)],
        compiler_params=_params("parallel", "parallel"),
        name="mixer_ab",
    )(a, v, a, v,
      conv_w, conv_b.reshape(1, dc), ln_g.reshape(1, dc), ln_b.reshape(1, dc),
      pool_w.astype(BF16), pool_scale.reshape(1, dp))


def _gate_conv_kernel(gb_ref, z_ref, hz_ref, cw_ref, u_ref, zbuf):
    s = pl.program_id(1)
    ts = z_ref.shape[1]
    taps = cw_ref.shape[0]
    zbuf[0:SHORT_HALO_ROWS, :] = jnp.where(s == 0, 0.0, hz_ref[0].astype(F32))
    zbuf[SHORT_HALO_ROWS:, :] = z_ref[0].astype(F32)
    base = SHORT_HALO_ROWS - (taps - 1)
    conv = cw_ref[0:1, :] * zbuf[pl.ds(base, ts), :]
    for k in range(1, taps):
        conv = conv + cw_ref[k:k + 1, :] * zbuf[pl.ds(base + k, ts), :]
    u_ref[0] = (gb_ref[0].astype(F32) * conv).astype(u_ref.dtype)


def _gate_conv(gb, z, conv_w):
    b, s, ds = z.shape
    taps = conv_w.shape[0]
    assert taps - 1 <= SHORT_HALO_ROWS
    ts = _tile(s, 256)
    halo_per_tile = ts // SHORT_HALO_ROWS
    tile = pl.BlockSpec((1, ts, ds), lambda i, j: (i, j, 0))
    return pl.pallas_call(
        _gate_conv_kernel,
        grid=(b, s // ts),
        in_specs=[tile, tile,
                  pl.BlockSpec((1, SHORT_HALO_ROWS, ds),
                               lambda i, j: (i, jnp.maximum(j * halo_per_tile - 1, 0), 0)),
                  pl.BlockSpec((taps, ds), lambda i, j: (0, 0))],
        out_specs=tile,
        out_shape=jax.ShapeDtypeStruct((b, s, ds), BF16),
        scratch_shapes=[pltpu.VMEM((SHORT_HALO_ROWS + ts, ds), F32)],
        compiler_params=_params("parallel", "parallel"),
        name="gate_conv",
    )(gb, z, z, conv_w)


def _ffn_kernel(x_ref, g_ref, sh_ref, sc_ref, gate_ref, fg_ref, w1_ref, w3_ref, w2_ref, o_ref, h_scr,
                *, final_norm):
    f = pl.program_id(1)

    @pl.when(f == 0)
    def _():
        _normmod_rows(x_ref, g_ref, sh_ref, sc_ref, h_scr)
        o_ref[...] = jnp.zeros_like(o_ref)

    h = h_scr[...]
    a = jnp.dot(h, w1_ref[...], preferred_element_type=F32)
    b = jnp.dot(h, w3_ref[...], preferred_element_type=F32)
    u = (_silu(a) * b).astype(BF16)
    o_ref[...] += jnp.dot(u, w2_ref[...], preferred_element_type=F32)

    @pl.when(f == pl.num_programs(1) - 1)
    def _():
        gate = gate_ref[0]
        fg = fg_ref[...]
        chunk = _tile(x_ref.shape[0], NORM_CHUNK)

        def body(i, carry):
            rows = pl.ds(pl.multiple_of(i * chunk, chunk), chunk)
            xn = x_ref[rows, :] + gate * o_ref[rows, :]
            o_ref[rows, :] = _rms(xn, fg) if final_norm else xn
            return carry

        lax.fori_loop(0, x_ref.shape[0] // chunk, body, 0)


def _ffn(x, g, shift, scale, gate, final_g, w1, w3, w2, *, tiles_per_batch, final_norm):
    m, d = x.shape
    dff = w1.shape[1]
    bm = m // (gate.shape[0] * tiles_per_batch)
    bf = _tile(dff, 256)
    per_batch = _per_batch_spec(d, tiles_per_batch)
    shared = pl.BlockSpec((1, d), lambda i, j: (0, 0))
    return pl.pallas_call(
        functools.partial(_ffn_kernel, final_norm=final_norm),
        grid=(m // bm, dff // bf),
        in_specs=[pl.BlockSpec((bm, d), lambda i, j: (i, 0), pipeline_mode=ONCE),
                  shared, per_batch, per_batch, per_batch, shared,
                  pl.BlockSpec((d, bf), lambda i, j: (0, j)),
                  pl.BlockSpec((d, bf), lambda i, j: (0, j)),
                  pl.BlockSpec((bf, d), lambda i, j: (j, 0))],
        out_specs=pl.BlockSpec((bm, d), lambda i, j: (i, 0), pipeline_mode=ONCE),
        out_shape=jax.ShapeDtypeStruct((m, d), F32),
        scratch_shapes=[pltpu.VMEM((bm, d), BF16)],
        compiler_params=_params("parallel", "arbitrary"),
        name="ffn_final" if final_norm else "ffn",
    )(x, g.reshape(1, d), shift, scale, gate, final_g.reshape(1, d), w1, w3, w2)


def kernel(x, c, mod_w, mod_b, norm_mix_g, norm_ffn_g, ab_in_w, conv_a_w, conv_a_b, ln_a_g, ln_a_b,
           pool_w, pool_scale, ab_out_w, c_in_w, conv_c_w, c_out_w, ffn_w1, ffn_w3, ffn_w2, final_g):
    b, s, d = x.shape
    depth = mod_w.shape[0]
    m = b * s
    tiles_per_batch = s // _tile(s, ROW_TILE)
    mod = _modulation(c, mod_w, mod_b)

    def mod_part(i, part):
        return mod[i, :, part * d:(part + 1) * d].reshape(b, 1, d)

    x = x.reshape(m, d)
    for i in range(depth):
        j = i // 2
        sh_m, sc_m, g_m, sh_f, sc_f, g_f = (mod_part(i, p) for p in range(6))
        if i % 2 == 0:
            a, v = _proj3(x, norm_mix_g[i], sh_m, sc_m, ab_in_w[j].astype(BF16),
                          tiles_per_batch=tiles_per_batch, glu=True, out_dtype=F32)
            u = _mixer_ab(a.reshape(b, s, -1), v.reshape(b, s, -1), conv_a_w[j], conv_a_b[j],
                          ln_a_g[j], ln_a_b[j], pool_w[j], pool_scale[j])
            x = _matmul_residual(u.reshape(m, -1), ab_out_w[j].astype(BF16), x, g_m,
                                 tiles_per_batch=tiles_per_batch)
        else:
            gb, z = _proj3(x, norm_mix_g[i], sh_m, sc_m, c_in_w[j].astype(BF16),
                           tiles_per_batch=tiles_per_batch, glu=False, out_dtype=BF16)
            u = _gate_conv(gb.reshape(b, s, -1), z.reshape(b, s, -1), conv_c_w[j])
            x = _matmul_residual(u.reshape(m, -1), c_out_w[j].astype(BF16), x, g_m,
                                 tiles_per_batch=tiles_per_batch)
        x = _ffn(x, norm_ffn_g[i], sh_f, sc_f, g_f, final_g,
                 ffn_w1[i].astype(BF16), ffn_w3[i].astype(BF16), ffn_w2[i].astype(BF16),
                 tiles_per_batch=tiles_per_batch, final_norm=(i + 1 == depth))
    return x.reshape(b, s, d)
```

```python
import functools

import jax
import jax.numpy as jnp
from jax import lax
from jax.experimental import pallas as pl
from jax.experimental.pallas import tpu as pltpu

EPS = 1e-6
POOL_WINDOWS = (2, 4, 8, 16)

F32 = jnp.float32
BF16 = jnp.bfloat16

V7X_VMEM_BYTES = 64 * 1024 * 1024
V7X_LANES = 128
V7X_SUBLANES = 8
V7X_BF16_SUBLANES = 16
VMEM_LIMIT_BYTES = V7X_VMEM_BYTES - 6 * 1024 * 1024

ROW_TILE = 1024
NORM_CHUNK = 64
HALO_ROWS = 32
SHORT_HALO_ROWS = V7X_SUBLANES
PROJ_COLS = 256
CAST_BLOCK_BYTES = 8 * 1024 * 1024
CONV_PARTIAL_SUMS = 4

ONCE = pl.Buffered(1)


def _params(*semantics):
    return pltpu.CompilerParams(dimension_semantics=semantics, vmem_limit_bytes=VMEM_LIMIT_BYTES)


def _tile(n, want):
    t = min(n, want)
    while n % t:
        t //= 2
    return t


def _silu(v):
    return v * jax.nn.sigmoid(v)


def _rms(x, g):
    ms = jnp.mean(x * x, axis=-1, keepdims=True)
    return x * lax.rsqrt(ms + EPS) * g


def _cast_kernel(w_ref, o_ref):
    o_ref[...] = w_ref[0].astype(o_ref.dtype)


def _to_bf16(w, layer):
    _, k, n = w.shape
    want = max(V7X_BF16_SUBLANES, CAST_BLOCK_BYTES // (4 * n))
    bk = _tile(k, 1 << (want.bit_length() - 1))
    return pl.pallas_call(
        _cast_kernel,
        grid=(k // bk,),
        in_specs=[pl.BlockSpec((1, bk, n), lambda i: (layer, i, 0))],
        out_specs=pl.BlockSpec((bk, n), lambda i: (i, 0)),
        out_shape=jax.ShapeDtypeStruct((k, n), BF16),
        compiler_params=_params("parallel"),
        name="to_bf16",
    )(w)


def _mod_kernel(c_ref, w_ref, b_ref, o_ref):
    ca = _silu(c_ref[...]).astype(BF16)
    w = w_ref[0].astype(BF16)
    o_ref[0] = jnp.dot(ca, w, preferred_element_type=F32) + b_ref[0]


def _modulation(c, mod_w, mod_b):
    depth, d, n = mod_w.shape
    b = c.shape[0]
    rows = V7X_SUBLANES
    c_pad = jnp.zeros((rows, d), F32).at[:b].set(c)
    tn = _tile(n, 512)
    out = pl.pallas_call(
        _mod_kernel,
        grid=(depth, n // tn),
        in_specs=[
            pl.BlockSpec((rows, d), lambda i, j: (0, 0)),
            pl.BlockSpec((1, d, tn), lambda i, j: (i, 0, j)),
            pl.BlockSpec((1, 1, tn), lambda i, j: (i, 0, j)),
        ],
        out_specs=pl.BlockSpec((1, rows, tn), lambda i, j: (i, 0, j)),
        out_shape=jax.ShapeDtypeStruct((depth, rows, n), F32),
        compiler_params=_params("arbitrary", "arbitrary"),
        name="modulation",
    )(c_pad, mod_w, mod_b.reshape(depth, 1, n))
    return out[:, :b, :]


def _normmod_rows(x_ref, g_ref, sh_ref, sc_ref, h_scr):
    w = g_ref[...] * (1.0 + sc_ref[0])
    shift = sh_ref[0]
    chunk = _tile(x_ref.shape[0], NORM_CHUNK)

    def body(i, carry):
        rows = pl.ds(pl.multiple_of(i * chunk, chunk), chunk)
        h_scr[rows, :] = (_rms(x_ref[rows, :], w) + shift).astype(h_scr.dtype)
        return carry

    lax.fori_loop(0, x_ref.shape[0] // chunk, body, 0)


def _per_batch_spec(d, tiles_per_batch):
    return pl.BlockSpec((1, 1, d), lambda i, j: (i // tiles_per_batch, 0, 0))


def _three_dots(x_ref, g_ref, sh_ref, sc_ref, w0_ref, w1_ref, w2_ref, h_scr):
    @pl.when(pl.program_id(1) == 0)
    def _():
        _normmod_rows(x_ref, g_ref, sh_ref, sc_ref, h_scr)

    h = h_scr[...]
    return tuple(jnp.dot(h, w_ref[...], preferred_element_type=F32) for w_ref in (w0_ref, w1_ref, w2_ref))


def _proj_glu_kernel(x_ref, g_ref, sh_ref, sc_ref, w0_ref, w1_ref, w2_ref, a_ref, v_ref, h_scr):
    a_val, a_gate, b_in = _three_dots(x_ref, g_ref, sh_ref, sc_ref, w0_ref, w1_ref, w2_ref, h_scr)
    a_ref[...] = a_val * jax.nn.sigmoid(a_gate)
    v_ref[...] = b_in


def _proj_gate_kernel(x_ref, g_ref, sh_ref, sc_ref, w0_ref, w1_ref, w2_ref, cw_ref, u_ref,
                      h_scr, zbuf, tail_scr, *, tiles_per_batch):
    gate_b, gate_c, v = _three_dots(x_ref, g_ref, sh_ref, sc_ref, w0_ref, w1_ref, w2_ref, h_scr)
    n = pl.program_id(1)
    bm = u_ref.shape[0]
    taps = cw_ref.shape[0]
    hist = zbuf.shape[0] - bm
    z = gate_c * v
    starts_sequence = pl.program_id(0) % tiles_per_batch == 0

    @pl.when(starts_sequence)
    def _():
        zbuf[0:hist, :] = jnp.zeros((hist, zbuf.shape[1]), F32)

    @pl.when(jnp.logical_not(starts_sequence))
    def _():
        zbuf[0:hist, :] = tail_scr[n]

    zbuf[hist:, :] = z
    tail_scr[n] = z[bm - hist:, :]
    base = hist - (taps - 1)
    conv = cw_ref[0:1, :] * zbuf[pl.ds(base, bm), :]
    for k in range(1, taps):
        conv = conv + cw_ref[k:k + 1, :] * zbuf[pl.ds(base + k, bm), :]
    u_ref[...] = (gate_b * conv).astype(u_ref.dtype)


def _proj_call(kernel_fn, name, x, g, shift, scale, w, extra_in, extra_specs, out_dtypes, scratch,
               *, tiles_per_batch, bn):
    m, d = x.shape
    n = w.shape[1] // 3
    bm = m // (shift.shape[0] * tiles_per_batch)
    nb = n // bn

    def wspec(group):
        return pl.BlockSpec((d, bn), lambda i, j: (0, j + group * nb))

    per_batch = _per_batch_spec(d, tiles_per_batch)
    out = pl.BlockSpec((bm, bn), lambda i, j: (i, j))
    return pl.pallas_call(
        kernel_fn,
        grid=(m // bm, nb),
        in_specs=[pl.BlockSpec((bm, d), lambda i, j: (i, 0), pipeline_mode=ONCE),
                  pl.BlockSpec((1, d), lambda i, j: (0, 0)), per_batch, per_batch,
                  wspec(0), wspec(1), wspec(2)] + extra_specs,
        out_specs=[out] * len(out_dtypes),
        out_shape=[jax.ShapeDtypeStruct((m, n), dt) for dt in out_dtypes],
        scratch_shapes=[pltpu.VMEM((bm, d), BF16)] + scratch(bm, nb),
        compiler_params=_params("arbitrary", "arbitrary"),
        name=name,
    )(x, g.reshape(1, d), shift, scale, w, w, w, *extra_in)


def _proj_glu(x, g, shift, scale, w, *, tiles_per_batch):
    bn = _tile(w.shape[1] // 3, PROJ_COLS)
    return _proj_call(_proj_glu_kernel, "proj_glu", x, g, shift, scale, w, [], [], [F32, F32],
                      lambda bm, nb: [], tiles_per_batch=tiles_per_batch, bn=bn)


def _proj_gate(x, g, shift, scale, w, conv_w, *, tiles_per_batch):
    taps = conv_w.shape[0]
    assert taps - 1 <= SHORT_HALO_ROWS
    bn = _tile(w.shape[1] // 3, PROJ_COLS)
    (u,) = _proj_call(
        functools.partial(_proj_gate_kernel, tiles_per_batch=tiles_per_batch), "proj_gate",
        x, g, shift, scale, w, [conv_w], [pl.BlockSpec((taps, bn), lambda i, j: (0, j))], [BF16],
        lambda bm, nb: [pltpu.VMEM((SHORT_HALO_ROWS + bm, bn), F32),
                        pltpu.VMEM((nb, SHORT_HALO_ROWS, bn), F32)],
        tiles_per_batch=tiles_per_batch, bn=bn)
    return u


def _mm_res_kernel(u_ref, w_ref, x_ref, gate_ref, o_ref):
    y = jnp.dot(u_ref[...], w_ref[...], preferred_element_type=F32)
    o_ref[...] = x_ref[...] + gate_ref[0] * y


def _matmul_residual(u, w, x, gate, *, tiles_per_batch):
    m, k = u.shape
    n = w.shape[1]
    bm = m // (gate.shape[0] * tiles_per_batch)
    bn = _tile(n, 1024)
    tile = pl.BlockSpec((bm, bn), lambda i, j: (i, j))
    return pl.pallas_call(
        _mm_res_kernel,
        grid=(m // bm, n // bn),
        in_specs=[pl.BlockSpec((bm, k), lambda i, j: (i, 0)),
                  pl.BlockSpec((k, bn), lambda i, j: (0, j)),
                  tile,
                  pl.BlockSpec((1, 1, bn), lambda i, j: (i // tiles_per_batch, 0, j))],
        out_specs=tile,
        out_shape=jax.ShapeDtypeStruct((m, n), F32),
        compiler_params=_params("parallel", "arbitrary"),
        name="matmul_residual",
    )(u, w, x, gate)


def _mixer_ab_kernel(a_ref, v_ref, ha_ref, hv_ref, cw_ref, cb_ref, lg_ref, lb_ref, pw_ref, ps_ref,
                     o_ref, abuf, shifted, ybuf, vbuf, pbuf):
    s = pl.program_id(1)
    ts = a_ref.shape[1]
    dc = a_ref.shape[2]
    dp = v_ref.shape[2]
    taps = cw_ref.shape[0]
    n_groups = pw_ref.shape[0]
    dg = dp // n_groups
    first = s == 0

    abuf[0:HALO_ROWS, 0:dc] = jnp.where(first, 0.0, ha_ref[0])
    abuf[HALO_ROWS:, 0:dc] = a_ref[0]
    vbuf[0:HALO_ROWS, :] = jnp.where(first, 0.0, hv_ref[0])
    vbuf[HALO_ROWS:, :] = v_ref[0]

    base = HALO_ROWS - (taps - 1)
    rows = V7X_SUBLANES
    n_shift = shifted.shape[1]
    for r in range(1, rows):
        shifted[r - 1, :, 0:dc] = abuf[pl.ds(r, n_shift), 0:dc]

    def tap(k, r0, lanes):
        q, r = divmod(base + k, rows)
        start = pl.multiple_of(r0 + q * rows, rows)
        if r == 0:
            return abuf[pl.ds(start, rows), lanes]
        return shifted[r - 1, pl.ds(start, rows), lanes]

    for c in range(dc // V7X_LANES):
        lanes = pl.ds(c * V7X_LANES, V7X_LANES)
        wk = [jnp.broadcast_to(cw_ref[k:k + 1, lanes], (rows, V7X_LANES)) for k in range(taps)]
        bias = jnp.broadcast_to(cb_ref[:, lanes], (rows, V7X_LANES))

        def row_body(i, carry, lanes=lanes, wk=wk, bias=bias):
            r0 = pl.multiple_of(i * rows, rows)
            parts = [wk[k] * tap(k, r0, lanes) for k in range(CONV_PARTIAL_SUMS)]
            for k in range(CONV_PARTIAL_SUMS, taps):
                parts[k % CONV_PARTIAL_SUMS] = parts[k % CONV_PARTIAL_SUMS] + wk[k] * tap(k, r0, lanes)
            while len(parts) > 1:
                parts = [p + q for p, q in zip(parts[0::2], parts[1::2])] + parts[len(parts) & ~1:]
            ybuf[pl.ds(r0, rows), lanes] = parts[0] + bias
            return carry

        lax.fori_loop(0, ts // rows, row_body, 0, unroll=4)

    y = ybuf[...]
    mu = jnp.mean(y, axis=-1, keepdims=True)
    yc = y - mu
    var = jnp.mean(yc * yc, axis=-1, keepdims=True)
    a_out = _silu(yc * lax.rsqrt(var + EPS) * lg_ref[...] + lb_ref[...])
    o_ref[0, :, 0:dc] = a_out.astype(o_ref.dtype)

    t1 = (s * ts + 1 + lax.broadcasted_iota(jnp.int32, (ts, dg), 0)).astype(F32)
    for g, w in enumerate(POOL_WINDOWS[:n_groups]):
        lanes = pl.ds(g * dg, dg)
        v = vbuf[pl.ds(HALO_ROWS, ts), lanes]
        src, src_lanes, d, stage = vbuf, lanes, 1, 0
        while 2 * d < w:
            lo = (stage + 1) * V7X_SUBLANES
            n = HALO_ROWS + ts - lo
            dst = pbuf.at[stage % 2]
            dst[pl.ds(lo, n), :] = src[pl.ds(lo, n), src_lanes] + src[pl.ds(lo - d, n), src_lanes]
            src, src_lanes, d, stage = dst, slice(None), 2 * d, stage + 1
        acc = src[pl.ds(HALO_ROWS, ts), src_lanes] + src[pl.ds(HALO_ROWS - d, ts), src_lanes]
        pooled = acc / jnp.minimum(t1, float(w)) - v
        mixed = jnp.dot(pooled.astype(BF16), pw_ref[g], preferred_element_type=F32)
        o_ref[0, :, pl.ds(dc + g * dg, dg)] = (mixed * ps_ref[:, lanes]).astype(o_ref.dtype)


def _mixer_ab(a, v, conv_w, conv_b, ln_g, ln_b, pool_w, pool_scale):
    b, s, dc = a.shape
    dp = v.shape[2]
    taps = conv_w.shape[0]
    assert taps - 1 <= HALO_ROWS
    assert all(w & (w - 1) == 0 for w in POOL_WINDOWS) and 2 * max(POOL_WINDOWS) <= HALO_ROWS
    ts = _tile(s, 256)
    halo_per_tile = ts // HALO_ROWS

    def cur(width):
        return pl.BlockSpec((1, ts, width), lambda i, j: (i, j, 0))

    def halo(width):
        return pl.BlockSpec((1, HALO_ROWS, width),
                            lambda i, j: (i, jnp.maximum(j * halo_per_tile - 1, 0), 0))

    def whole(shape):
        return pl.BlockSpec(shape, lambda i, j: (0,) * len(shape))

    return pl.pallas_call(
        _mixer_ab_kernel,
        grid=(b, s // ts),
        in_specs=[cur(dc), cur(dp), halo(dc), halo(dp),
                  whole((taps, dc)), whole((1, dc)), whole((1, dc)), whole((1, dc)),
                  whole(pool_w.shape), whole((1, dp))],
        out_specs=pl.BlockSpec((1, ts, dc + dp), lambda i, j: (i, j, 0)),
        out_shape=jax.ShapeDtypeStruct((b, s, dc + dp), BF16),
        scratch_shapes=[pltpu.VMEM((HALO_ROWS + ts, dc + V7X_LANES), F32),
                        pltpu.VMEM((V7X_SUBLANES - 1, HALO_ROWS + ts - V7X_SUBLANES, dc + V7X_LANES), F32),
                        pltpu.VMEM((ts, dc), F32),
                        pltpu.VMEM((HALO_ROWS + ts, dp), F32),
                        pltpu.VMEM((2, HALO_ROWS + ts, dp // pool_w.shape[0]), F32)],
        compiler_params=_params("parallel", "parallel"),
        name="mixer_ab",
    )(a, v, a, v,
      conv_w, conv_b.reshape(1, dc), ln_g.reshape(1, dc), ln_b.reshape(1, dc),
      pool_w.astype(BF16), pool_scale.reshape(1, dp))


def _ffn_kernel(x_ref, g_ref, sh_ref, sc_ref, gate_ref, fg_ref, w1_ref, w3_ref, w2_ref, o_ref, h_scr,
                *, final_norm):
    f = pl.program_id(1)

    @pl.when(f == 0)
    def _():
        _normmod_rows(x_ref, g_ref, sh_ref, sc_ref, h_scr)
        o_ref[...] = jnp.zeros_like(o_ref)

    h = h_scr[...]
    a = jnp.dot(h, w1_ref[...], preferred_element_type=F32)
    b = jnp.dot(h, w3_ref[...], preferred_element_type=F32)
    u = (_silu(a) * b).astype(BF16)
    o_ref[...] += jnp.dot(u, w2_ref[...], preferred_element_type=F32)

    @pl.when(f == pl.num_programs(1) - 1)
    def _():
        gate = gate_ref[0]
        fg = fg_ref[...]
        chunk = _tile(x_ref.shape[0], NORM_CHUNK)

        def body(i, carry):
            rows = pl.ds(pl.multiple_of(i * chunk, chunk), chunk)
            xn = x_ref[rows, :] + gate * o_ref[rows, :]
            o_ref[rows, :] = _rms(xn, fg) if final_norm else xn
            return carry

        lax.fori_loop(0, x_ref.shape[0] // chunk, body, 0)


def _ffn(x, g, shift, scale, gate, final_g, w1, w3, w2, *, tiles_per_batch, final_norm):
    m, d = x.shape
    dff = w1.shape[1]
    bm = m // (gate.shape[0] * tiles_per_batch)
    bf = _tile(dff, 256)
    per_batch = _per_batch_spec(d, tiles_per_batch)
    shared = pl.BlockSpec((1, d), lambda i, j: (0, 0))
    return pl.pallas_call(
        functools.partial(_ffn_kernel, final_norm=final_norm),
        grid=(m // bm, dff // bf),
        in_specs=[pl.BlockSpec((bm, d), lambda i, j: (i, 0), pipeline_mode=ONCE),
                  shared, per_batch, per_batch, per_batch, shared,
                  pl.BlockSpec((d, bf), lambda i, j: (0, j)),
                  pl.BlockSpec((d, bf), lambda i, j: (0, j)),
                  pl.BlockSpec((bf, d), lambda i, j: (j, 0))],
        out_specs=pl.BlockSpec((bm, d), lambda i, j: (i, 0), pipeline_mode=ONCE),
        out_shape=jax.ShapeDtypeStruct((m, d), F32),
        scratch_shapes=[pltpu.VMEM((bm, d), BF16)],
        compiler_params=_params("parallel", "arbitrary"),
        name="ffn_final" if final_norm else "ffn",
    )(x, g.reshape(1, d), shift, scale, gate, final_g.reshape(1, d), w1, w3, w2)


def kernel(x, c, mod_w, mod_b, norm_mix_g, norm_ffn_g, ab_in_w, conv_a_w, conv_a_b, ln_a_g, ln_a_b,
           pool_w, pool_scale, ab_out_w, c_in_w, conv_c_w, c_out_w, ffn_w1, ffn_w3, ffn_w2, final_g):
    b, s, d = x.shape
    depth = mod_w.shape[0]
    m = b * s
    tiles_per_batch = s // _tile(s, ROW_TILE)
    mod = _modulation(c, mod_w, mod_b)

    def mod_part(i, part):
        return mod[i, :, part * d:(part + 1) * d].reshape(b, 1, d)

    x = x.reshape(m, d)
    for i in range(depth):
        j = i // 2
        sh_m, sc_m, g_m, sh_f, sc_f, g_f = (mod_part(i, p) for p in range(6))
        if i % 2 == 0:
            a, v = _proj_glu(x, norm_mix_g[i], sh_m, sc_m, _to_bf16(ab_in_w, j),
                             tiles_per_batch=tiles_per_batch)
            u = _mixer_ab(a.reshape(b, s, -1), v.reshape(b, s, -1), conv_a_w[j], conv_a_b[j],
                          ln_a_g[j], ln_a_b[j], pool_w[j], pool_scale[j])
            x = _matmul_residual(u.reshape(m, -1), _to_bf16(ab_out_w, j), x, g_m,
                                 tiles_per_batch=tiles_per_batch)
        else:
            u = _proj_gate(x, norm_mix_g[i], sh_m, sc_m, _to_bf16(c_in_w, j), conv_c_w[j],
                           tiles_per_batch=tiles_per_batch)
            x = _matmul_residual(u, _to_bf16(c_out_w, j), x, g_m, tiles_per_batch=tiles_per_batch)
        x = _ffn(x, norm_ffn_g[i], sh_f, sc_f, g_f, final_g,
                 _to_bf16(ffn_w1, i), _to_bf16(ffn_w3, i), _to_bf16(ffn_w2, i),
                 tiles_per_batch=tiles_per_batch, final_norm=(i + 1 == depth))
    return x.reshape(b, s, d)
```

```python
import functools

import jax
import jax.numpy as jnp
from jax import lax
from jax.experimental import pallas as pl
from jax.experimental.pallas import tpu as pltpu

EPS = 1e-6
POOL_WINDOWS = (2, 4, 8, 16)

F32 = jnp.float32
BF16 = jnp.bfloat16

V7X_VMEM_BYTES = 64 * 1024 * 1024
V7X_LANES = 128
V7X_SUBLANES = 8
V7X_BF16_SUBLANES = 16
VMEM_LIMIT_BYTES = V7X_VMEM_BYTES - 6 * 1024 * 1024

ROW_TILE = 1024
NORM_CHUNK = 64
HALO_ROWS = 32
SHORT_HALO_ROWS = V7X_SUBLANES
PROJ_COLS = 256
CAST_BLOCK_BYTES = 8 * 1024 * 1024
CONV_PARTIAL_SUMS = 4

ONCE = pl.Buffered(1)


def _params(*semantics):
    return pltpu.CompilerParams(dimension_semantics=semantics, vmem_limit_bytes=VMEM_LIMIT_BYTES)


def _tile(n, want):
    t = min(n, want)
    while n % t:
        t //= 2
    return t


def _silu(v):
    return v * jax.nn.sigmoid(v)


def _rms(x, g):
    ms = jnp.mean(x * x, axis=-1, keepdims=True)
    return x * lax.rsqrt(ms + EPS) * g


def _cast_kernel(w_ref, o_ref):
    o_ref[...] = w_ref[0].astype(o_ref.dtype)


def _to_bf16(w, layer):
    _, k, n = w.shape
    want = max(V7X_BF16_SUBLANES, CAST_BLOCK_BYTES // (4 * n))
    bk = _tile(k, 1 << (want.bit_length() - 1))
    return pl.pallas_call(
        _cast_kernel,
        grid=(k // bk,),
        in_specs=[pl.BlockSpec((1, bk, n), lambda i: (layer, i, 0))],
        out_specs=pl.BlockSpec((bk, n), lambda i: (i, 0)),
        out_shape=jax.ShapeDtypeStruct((k, n), BF16),
        compiler_params=_params("parallel"),
        name="to_bf16",
    )(w)


def _mod_kernel(c_ref, w_ref, b_ref, o_ref):
    ca = _silu(c_ref[...]).astype(BF16)
    w = w_ref[0].astype(BF16)
    o_ref[0] = jnp.dot(ca, w, preferred_element_type=F32) + b_ref[0]


def _modulation(c, mod_w, mod_b):
    depth, d, n = mod_w.shape
    b = c.shape[0]
    rows = V7X_SUBLANES
    c_pad = jnp.zeros((rows, d), F32).at[:b].set(c)
    tn = _tile(n, 512)
    out = pl.pallas_call(
        _mod_kernel,
        grid=(depth, n // tn),
        in_specs=[
            pl.BlockSpec((rows, d), lambda i, j: (0, 0)),
            pl.BlockSpec((1, d, tn), lambda i, j: (i, 0, j)),
            pl.BlockSpec((1, 1, tn), lambda i, j: (i, 0, j)),
        ],
        out_specs=pl.BlockSpec((1, rows, tn), lambda i, j: (i, 0, j)),
        out_shape=jax.ShapeDtypeStruct((depth, rows, n), F32),
        compiler_params=_params("arbitrary", "arbitrary"),
        name="modulation",
    )(c_pad, mod_w, mod_b.reshape(depth, 1, n))
    return out[:, :b, :]


def _normmod_rows(x_ref, g_ref, sh_ref, sc_ref, h_scr):
    w = g_ref[...] * (1.0 + sc_ref[0])
    shift = sh_ref[0]
    chunk = _tile(x_ref.shape[0], NORM_CHUNK)

    def body(i, carry):
        rows = pl.ds(pl.multiple_of(i * chunk, chunk), chunk)
        h_scr[rows, :] = (_rms(x_ref[rows, :], w) + shift).astype(h_scr.dtype)
        return carry

    lax.fori_loop(0, x_ref.shape[0] // chunk, body, 0)


def _per_batch_spec(d, tiles_per_batch):
    return pl.BlockSpec((1, 1, d), lambda i, j: (i // tiles_per_batch, 0, 0))


def _three_dots(x_ref, g_ref, sh_ref, sc_ref, w0_ref, w1_ref, w2_ref, h_scr):
    @pl.when(pl.program_id(1) == 0)
    def _():
        _normmod_rows(x_ref, g_ref, sh_ref, sc_ref, h_scr)

    h = h_scr[...]
    return tuple(jnp.dot(h, w_ref[...], preferred_element_type=F32) for w_ref in (w0_ref, w1_ref, w2_ref))


def _proj_glu_kernel(x_ref, g_ref, sh_ref, sc_ref, w0_ref, w1_ref, w2_ref, a_ref, v_ref, h_scr):
    a_val, a_gate, b_in = _three_dots(x_ref, g_ref, sh_ref, sc_ref, w0_ref, w1_ref, w2_ref, h_scr)
    a_ref[...] = a_val * jax.nn.sigmoid(a_gate)
    v_ref[...] = b_in


def _proj_gate_kernel(x_ref, g_ref, sh_ref, sc_ref, w0_ref, w1_ref, w2_ref, cw_ref, u_ref,
                      h_scr, zbuf, tail_scr, *, tiles_per_batch):
    gate_b, gate_c, v = _three_dots(x_ref, g_ref, sh_ref, sc_ref, w0_ref, w1_ref, w2_ref, h_scr)
    n = pl.program_id(1)
    bm = u_ref.shape[0]
    taps = cw_ref.shape[0]
    hist = zbuf.shape[0] - bm
    z = gate_c * v
    starts_sequence = pl.program_id(0) % tiles_per_batch == 0

    @pl.when(starts_sequence)
    def _():
        zbuf[0:hist, :] = jnp.zeros((hist, zbuf.shape[1]), F32)

    @pl.when(jnp.logical_not(starts_sequence))
    def _():
        zbuf[0:hist, :] = tail_scr[n]

    zbuf[hist:, :] = z
    tail_scr[n] = z[bm - hist:, :]
    base = hist - (taps - 1)
    conv = cw_ref[0:1, :] * zbuf[pl.ds(base, bm), :]
    for k in range(1, taps):
        conv = conv + cw_ref[k:k + 1, :] * zbuf[pl.ds(base + k, bm), :]
    u_ref[...] = (gate_b * conv).astype(u_ref.dtype)


def _proj_call(kernel_fn, name, x, g, shift, scale, w, extra_in, extra_specs, out_dtypes, scratch,
               *, tiles_per_batch, bn):
    m, d = x.shape
    n = w.shape[1] // 3
    bm = m // (shift.shape[0] * tiles_per_batch)
    nb = n // bn

    def wspec(group):
        return pl.BlockSpec((d, bn), lambda i, j: (0, j + group * nb))

    per_batch = _per_batch_spec(d, tiles_per_batch)
    out = pl.BlockSpec((bm, bn), lambda i, j: (i, j))
    return pl.pallas_call(
        kernel_fn,
        grid=(m // bm, nb),
        in_specs=[pl.BlockSpec((bm, d), lambda i, j: (i, 0), pipeline_mode=ONCE),
                  pl.BlockSpec((1, d), lambda i, j: (0, 0)), per_batch, per_batch,
                  wspec(0), wspec(1), wspec(2)] + extra_specs,
        out_specs=[out] * len(out_dtypes),
        out_shape=[jax.ShapeDtypeStruct((m, n), dt) for dt in out_dtypes],
        scratch_shapes=[pltpu.VMEM((bm, d), BF16)] + scratch(bm, nb),
        compiler_params=_params("arbitrary", "arbitrary"),
        name=name,
    )(x, g.reshape(1, d), shift, scale, w, w, w, *extra_in)


def _proj_glu(x, g, shift, scale, w, *, tiles_per_batch):
    bn = _tile(w.shape[1] // 3, PROJ_COLS)
    return _proj_call(_proj_glu_kernel, "proj_glu", x, g, shift, scale, w, [], [], [F32, F32],
                      lambda bm, nb: [], tiles_per_batch=tiles_per_batch, bn=bn)


def _proj_gate(x, g, shift, scale, w, conv_w, *, tiles_per_batch):
    taps = conv_w.shape[0]
    assert taps - 1 <= SHORT_HALO_ROWS
    bn = _tile(w.shape[1] // 3, PROJ_COLS)
    (u,) = _proj_call(
        functools.partial(_proj_gate_kernel, tiles_per_batch=tiles_per_batch), "proj_gate",
        x, g, shift, scale, w, [conv_w], [pl.BlockSpec((taps, bn), lambda i, j: (0, j))], [BF16],
        lambda bm, nb: [pltpu.VMEM((SHORT_HALO_ROWS + bm, bn), F32),
                        pltpu.VMEM((nb, SHORT_HALO_ROWS, bn), F32)],
        tiles_per_batch=tiles_per_batch, bn=bn)
    return u


def _mm_res_kernel(u_ref, w_ref, x_ref, gate_ref, o_ref):
    y = jnp.dot(u_ref[...], w_ref[...], preferred_element_type=F32)
    o_ref[...] = x_ref[...] + gate_ref[0] * y


def _matmul_residual(u, w, x, gate, *, tiles_per_batch):
    m, k = u.shape
    n = w.shape[1]
    bm = m // (gate.shape[0] * tiles_per_batch)
    bn = _tile(n, 1024)
    tile = pl.BlockSpec((bm, bn), lambda i, j: (i, j))
    return pl.pallas_call(
        _mm_res_kernel,
        grid=(m // bm, n // bn),
        in_specs=[pl.BlockSpec((bm, k), lambda i, j: (i, 0)),
                  pl.BlockSpec((k, bn), lambda i, j: (0, j)),
                  tile,
                  pl.BlockSpec((1, 1, bn), lambda i, j: (i // tiles_per_batch, 0, j))],
        out_specs=tile,
        out_shape=jax.ShapeDtypeStruct((m, n), F32),
        compiler_params=_params("parallel", "arbitrary"),
        name="matmul_residual",
    )(u, w, x, gate)


def _mixer_ab_kernel(a_ref, v_ref, ha_ref, hv_ref, cw_ref, cb_ref, lg_ref, lb_ref, pw_ref, ps_ref,
                     o_ref, abuf, shifted, ybuf, vbuf, pbuf):
    s = pl.program_id(1)
    ts = a_ref.shape[1]
    dc = a_ref.shape[2]
    dp = v_ref.shape[2]
    taps = cw_ref.shape[0]
    n_groups = pw_ref.shape[0]
    dg = dp // n_groups
    first = s == 0

    abuf[0:HALO_ROWS, 0:dc] = jnp.where(first, 0.0, ha_ref[0])
    abuf[HALO_ROWS:, 0:dc] = a_ref[0]
    vbuf[0:HALO_ROWS, :] = jnp.where(first, 0.0, hv_ref[0])
    vbuf[HALO_ROWS:, :] = v_ref[0]

    base = HALO_ROWS - (taps - 1)
    rows = V7X_SUBLANES
    n_shift = shifted.shape[1]
    for r in range(1, rows):
        shifted[r - 1, :, 0:dc] = abuf[pl.ds(r, n_shift), 0:dc]

    def tap(k, r0, lanes):
        q, r = divmod(base + k, rows)
        start = pl.multiple_of(r0 + q * rows, rows)
        if r == 0:
            return abuf[pl.ds(start, rows), lanes]
        return shifted[r - 1, pl.ds(start, rows), lanes]

    for c in range(dc // V7X_LANES):
        lanes = pl.ds(c * V7X_LANES, V7X_LANES)
        wk = [jnp.broadcast_to(cw_ref[k:k + 1, lanes], (rows, V7X_LANES)) for k in range(taps)]
        bias = jnp.broadcast_to(cb_ref[:, lanes], (rows, V7X_LANES))

        def row_body(i, carry, lanes=lanes, wk=wk, bias=bias):
            r0 = pl.multiple_of(i * rows, rows)
            parts = [wk[k] * tap(k, r0, lanes) for k in range(CONV_PARTIAL_SUMS)]
            for k in range(CONV_PARTIAL_SUMS, taps):
                parts[k % CONV_PARTIAL_SUMS] = parts[k % CONV_PARTIAL_SUMS] + wk[k] * tap(k, r0, lanes)
            while len(parts) > 1:
                parts = [p + q for p, q in zip(parts[0::2], parts[1::2])] + parts[len(parts) & ~1:]
            ybuf[pl.ds(r0, rows), lanes] = parts[0] + bias
            return carry

        lax.fori_loop(0, ts // rows, row_body, 0, unroll=4)

    y = ybuf[...]
    mu = jnp.mean(y, axis=-1, keepdims=True)
    yc = y - mu
    var = jnp.mean(yc * yc, axis=-1, keepdims=True)
    a_out = _silu(yc * lax.rsqrt(var + EPS) * lg_ref[...] + lb_ref[...])
    o_ref[0, :, 0:dc] = a_out.astype(o_ref.dtype)

    t1 = (s * ts + 1 + lax.broadcasted_iota(jnp.int32, (ts, dg), 0)).astype(F32)
    for g, w in enumerate(POOL_WINDOWS[:n_groups]):
        lanes = pl.ds(g * dg, dg)
        v = vbuf[pl.ds(HALO_ROWS, ts), lanes]
        src, src_lanes, d, stage = vbuf, lanes, 1, 0
        while 2 * d < w:
            lo = (stage + 1) * V7X_SUBLANES
            n = HALO_ROWS + ts - lo
            dst = pbuf.at[stage % 2]
            dst[pl.ds(lo, n), :] = src[pl.ds(lo, n), src_lanes] + src[pl.ds(lo - d, n), src_lanes]
            src, src_lanes, d, stage = dst, slice(None), 2 * d, stage + 1
        acc = src[pl.ds(HALO_ROWS, ts), src_lanes] + src[pl.ds(HALO_ROWS - d, ts), src_lanes]
        pooled = acc / jnp.minimum(t1, float(w)) - v
        mixed = jnp.dot(pooled.astype(BF16), pw_ref[g], preferred_element_type=F32)
        o_ref[0, :, pl.ds(dc + g * dg, dg)] = (mixed * ps_ref[:, lanes]).astype(o_ref.dtype)


def _mixer_ab(a, v, conv_w, conv_b, ln_g, ln_b, pool_w, pool_scale):
    b, s, dc = a.shape
    dp = v.shape[2]
    taps = conv_w.shape[0]
    assert taps - 1 <= HALO_ROWS
    assert all(w & (w - 1) == 0 for w in POOL_WINDOWS) and 2 * max(POOL_WINDOWS) <= HALO_ROWS
    ts = _tile(s, 256)
    halo_per_tile = ts // HALO_ROWS

    def cur(width):
        return pl.BlockSpec((1, ts, width), lambda i, j: (i, j, 0))

    def halo(width):
        return pl.BlockSpec((1, HALO_ROWS, width),
                            lambda i, j: (i, jnp.maximum(j * halo_per_tile - 1, 0), 0))

    def whole(shape):
        return pl.BlockSpec(shape, lambda i, j: (0,) * len(shape))

    return pl.pallas_call(
        _mixer_ab_kernel,
        grid=(b, s // ts),
        in_specs=[cur(dc), cur(dp), halo(dc), halo(dp),
                  whole((taps, dc)), whole((1, dc)), whole((1, dc)), whole((1, dc)),
                  whole(pool_w.shape), whole((1, dp))],
        out_specs=pl.BlockSpec((1, ts, dc + dp), lambda i, j: (i, j, 0)),
        out_shape=jax.ShapeDtypeStruct((b, s, dc + dp), BF16),
        scratch_shapes=[pltpu.VMEM((HALO_ROWS + ts, dc + V7X_LANES), F32),
                        pltpu.VMEM((V7X_SUBLANES - 1, HALO_ROWS + ts - V7X_SUBLANES, dc + V7X_LANES), F32),
                        pltpu.VMEM((ts, dc), F32),
                        pltpu.VMEM((HALO_ROWS + ts, dp), F32),
                        pltpu.VMEM((2, HALO_ROWS + ts, dp // pool_w.shape[0]), F32)],
        compiler_params=_params("parallel", "parallel"),
        name="mixer_ab",
    )(a, v, a, v,
      conv_w, conv_b.reshape(1, dc), ln_g.reshape(1, dc), ln_b.reshape(1, dc),
      pool_w.astype(BF16), pool_scale.reshape(1, dp))


FFN_PREFETCH_STEP = 1


def _ffn_kernel(x_hbm, g_ref, sh_ref, sc_ref, gate_ref, fg_ref, w1_ref, w3_ref, w2_ref, o_hbm,
                acc, h_scr, in_sem, out_sem, *, final_norm):
    m, f = pl.program_id(0), pl.program_id(1)
    n_m, n_f = pl.num_programs(0), pl.num_programs(1)
    bm = acc.shape[1]
    slot = m % 2
    other = 1 - slot

    def load(tile, s):
        return pltpu.make_async_copy(x_hbm.at[pl.ds(tile * bm, bm), :], acc.at[s], in_sem.at[s])

    def store(tile, s):
        return pltpu.make_async_copy(acc.at[s], o_hbm.at[pl.ds(tile * bm, bm), :], out_sem.at[s])

    @pl.when(jnp.logical_and(m == 0, f == 0))
    def _():
        load(0, 0).start()

    @pl.when(f == 0)
    def _():
        load(m, slot).wait()
        _normmod_rows(acc.at[slot], g_ref, sh_ref, sc_ref, h_scr)

    @pl.when(f == FFN_PREFETCH_STEP)
    def _():
        @pl.when(m >= 1)
        def _():
            store(m - 1, other).wait()

        @pl.when(m + 1 < n_m)
        def _():
            load(m + 1, other).start()

    h = h_scr[...]
    a = jnp.dot(h, w1_ref[...], preferred_element_type=F32)
    b = jnp.dot(h, w3_ref[...], preferred_element_type=F32)
    u = (_silu(a) * b).astype(BF16)
    acc[slot] += gate_ref[0] * jnp.dot(u, w2_ref[...], preferred_element_type=F32)

    @pl.when(f == n_f - 1)
    def _():
        if final_norm:
            fg = fg_ref[...]
            chunk = _tile(bm, NORM_CHUNK)

            def body(i, carry):
                rows = pl.ds(pl.multiple_of(i * chunk, chunk), chunk)
                acc[slot, rows, :] = _rms(acc[slot, rows, :], fg)
                return carry

            lax.fori_loop(0, bm // chunk, body, 0)
        store(m, slot).start()

        @pl.when(m == n_m - 1)
        def _():
            store(m, slot).wait()


def _ffn(x, g, shift, scale, gate, final_g, w1, w3, w2, *, tiles_per_batch, final_norm):
    m, d = x.shape
    dff = w1.shape[1]
    bm = m // (gate.shape[0] * tiles_per_batch)
    bf = _tile(dff, 256)
    assert dff // bf > FFN_PREFETCH_STEP
    per_batch = _per_batch_spec(d, tiles_per_batch)
    shared = pl.BlockSpec((1, d), lambda i, j: (0, 0))
    return pl.pallas_call(
        functools.partial(_ffn_kernel, final_norm=final_norm),
        grid=(m // bm, dff // bf),
        in_specs=[pl.BlockSpec(memory_space=pl.ANY),
                  shared, per_batch, per_batch, per_batch, shared,
                  pl.BlockSpec((d, bf), lambda i, j: (0, j)),
                  pl.BlockSpec((d, bf), lambda i, j: (0, j)),
                  pl.BlockSpec((bf, d), lambda i, j: (j, 0))],
        out_specs=pl.BlockSpec(memory_space=pl.ANY),
        out_shape=jax.ShapeDtypeStruct((m, d), F32),
        scratch_shapes=[pltpu.VMEM((2, bm, d), F32), pltpu.VMEM((bm, d), BF16),
                        pltpu.SemaphoreType.DMA((2,)), pltpu.SemaphoreType.DMA((2,))],
        compiler_params=_params("arbitrary", "arbitrary"),
        name="ffn_final" if final_norm else "ffn",
    )(x, g.reshape(1, d), shift, scale, gate, final_g.reshape(1, d), w1, w3, w2)


def kernel(x, c, mod_w, mod_b, norm_mix_g, norm_ffn_g, ab_in_w, conv_a_w, conv_a_b, ln_a_g, ln_a_b,
           pool_w, pool_scale, ab_out_w, c_in_w, conv_c_w, c_out_w, ffn_w1, ffn_w3, ffn_w2, final_g):
    b, s, d = x.shape
    depth = mod_w.shape[0]
    m = b * s
    tiles_per_batch = s // _tile(s, ROW_TILE)
    mod = _modulation(c, mod_w, mod_b)

    def mod_part(i, part):
        return mod[i, :, part * d:(part + 1) * d].reshape(b, 1, d)

    x = x.reshape(m, d)
    for i in range(depth):
        j = i // 2
        sh_m, sc_m, g_m, sh_f, sc_f, g_f = (mod_part(i, p) for p in range(6))
        if i % 2 == 0:
            a, v = _proj_glu(x, norm_mix_g[i], sh_m, sc_m, _to_bf16(ab_in_w, j),
                             tiles_per_batch=tiles_per_batch)
            u = _mixer_ab(a.reshape(b, s, -1), v.reshape(b, s, -1), conv_a_w[j], conv_a_b[j],
                          ln_a_g[j], ln_a_b[j], pool_w[j], pool_scale[j])
            x = _matmul_residual(u.reshape(m, -1), _to_bf16(ab_out_w, j), x, g_m,
                                 tiles_per_batch=tiles_per_batch)
        else:
            u = _proj_gate(x, norm_mix_g[i], sh_m, sc_m, _to_bf16(c_in_w, j), conv_c_w[j],
                           tiles_per_batch=tiles_per_batch)
            x = _matmul_residual(u, _to_bf16(c_out_w, j), x, g_m, tiles_per_batch=tiles_per_batch)
        x = _ffn(x, norm_ffn_g[i], sh_f, sc_f, g_f, final_g,
                 _to_bf16(ffn_w1, i), _to_bf16(ffn_w3, i), _to_bf16(ffn_w2, i),
                 tiles_per_batch=tiles_per_batch, final_norm=(i + 1 == depth))
    return x.reshape(b, s, d)
```

```python
import functools

import jax
import jax.numpy as jnp
from jax import lax
from jax.experimental import pallas as pl
from jax.experimental.pallas import tpu as pltpu

EPS = 1e-6
POOL_WINDOWS = (2, 4, 8, 16)

F32 = jnp.float32
BF16 = jnp.bfloat16

V7X_VMEM_BYTES = 64 * 1024 * 1024
V7X_LANES = 128
V7X_SUBLANES = 8
V7X_BF16_SUBLANES = 16
VMEM_LIMIT_BYTES = V7X_VMEM_BYTES - 6 * 1024 * 1024

ROW_TILE = 1024
NORM_CHUNK = 64
HALO_ROWS = 32
SHORT_HALO_ROWS = V7X_SUBLANES
PROJ_COLS = 256
CAST_BLOCK_BYTES = 8 * 1024 * 1024
CONV_PARTIAL_SUMS = 4


def _params(*semantics):
    return pltpu.CompilerParams(dimension_semantics=semantics, vmem_limit_bytes=VMEM_LIMIT_BYTES)


def _tile(n, want):
    t = min(n, want)
    while n % t:
        t //= 2
    return t


def _silu(v):
    return v * jax.nn.sigmoid(v)


def _rms(x, g):
    ms = jnp.mean(x * x, axis=-1, keepdims=True)
    return x * lax.rsqrt(ms + EPS) * g


def _cast_kernel(w_ref, o_ref):
    o_ref[...] = w_ref[0].astype(o_ref.dtype)


def _to_bf16(w, layer):
    _, k, n = w.shape
    want = max(V7X_BF16_SUBLANES, CAST_BLOCK_BYTES // (4 * n))
    bk = _tile(k, 1 << (want.bit_length() - 1))
    return pl.pallas_call(
        _cast_kernel,
        grid=(k // bk,),
        in_specs=[pl.BlockSpec((1, bk, n), lambda i: (layer, i, 0))],
        out_specs=pl.BlockSpec((bk, n), lambda i: (i, 0)),
        out_shape=jax.ShapeDtypeStruct((k, n), BF16),
        compiler_params=_params("parallel"),
        name="to_bf16",
    )(w)


def _mod_kernel(c_ref, w_ref, b_ref, o_ref):
    ca = _silu(c_ref[...]).astype(BF16)
    w = w_ref[0].astype(BF16)
    o_ref[0] = jnp.dot(ca, w, preferred_element_type=F32) + b_ref[0]


def _modulation(c, mod_w, mod_b):
    depth, d, n = mod_w.shape
    b = c.shape[0]
    rows = V7X_SUBLANES
    c_pad = jnp.zeros((rows, d), F32).at[:b].set(c)
    tn = _tile(n, 512)
    out = pl.pallas_call(
        _mod_kernel,
        grid=(depth, n // tn),
        in_specs=[
            pl.BlockSpec((rows, d), lambda i, j: (0, 0)),
            pl.BlockSpec((1, d, tn), lambda i, j: (i, 0, j)),
            pl.BlockSpec((1, 1, tn), lambda i, j: (i, 0, j)),
        ],
        out_specs=pl.BlockSpec((1, rows, tn), lambda i, j: (i, 0, j)),
        out_shape=jax.ShapeDtypeStruct((depth, rows, n), F32),
        compiler_params=_params("arbitrary", "arbitrary"),
        name="modulation",
    )(c_pad, mod_w, mod_b.reshape(depth, 1, n))
    return out[:, :b, :]


def _normmod_rows(x_ref, g_ref, sh_ref, sc_ref, h_scr):
    w = g_ref[...] * (1.0 + sc_ref[0])
    shift = sh_ref[0]
    chunk = _tile(x_ref.shape[0], NORM_CHUNK)

    def body(i, carry):
        rows = pl.ds(pl.multiple_of(i * chunk, chunk), chunk)
        h_scr[rows, :] = (_rms(x_ref[rows, :], w) + shift).astype(h_scr.dtype)
        return carry

    lax.fori_loop(0, x_ref.shape[0] // chunk, body, 0)


def _per_batch_spec(d, tiles_per_batch):
    return pl.BlockSpec((1, 1, d), lambda i, j: (i // tiles_per_batch, 0, 0))


PROJ_PREFETCH_STEP = 1


def _glu_epilogue(parts, extra, outs, scratch, *, tiles_per_batch):
    a_val, a_gate, b_in = parts
    outs[0][...] = a_val * jax.nn.sigmoid(a_gate)
    outs[1][...] = b_in


def _gate_epilogue(parts, extra, outs, scratch, *, tiles_per_batch):
    gate_b, gate_c, v = parts
    (cw_ref,), (u_ref,), (zbuf, tail_scr) = extra, outs, scratch
    n = pl.program_id(1)
    bm = u_ref.shape[0]
    taps = cw_ref.shape[0]
    hist = zbuf.shape[0] - bm
    z = gate_c * v
    starts_sequence = pl.program_id(0) % tiles_per_batch == 0

    @pl.when(starts_sequence)
    def _():
        zbuf[0:hist, :] = jnp.zeros((hist, zbuf.shape[1]), F32)

    @pl.when(jnp.logical_not(starts_sequence))
    def _():
        zbuf[0:hist, :] = tail_scr[n]

    zbuf[hist:, :] = z
    tail_scr[n] = z[bm - hist:, :]
    base = hist - (taps - 1)
    conv = cw_ref[0:1, :] * zbuf[pl.ds(base, bm), :]
    for k in range(1, taps):
        conv = conv + cw_ref[k:k + 1, :] * zbuf[pl.ds(base + k, bm), :]
    u_ref[...] = (gate_b * conv).astype(u_ref.dtype)


def _proj_kernel(*refs, epilogue, n_extra, n_out, n_cast):
    x_hbm, g_ref, sh_ref, sc_ref = refs[:4]
    w_refs = refs[4:7]
    rest = list(refs[7:])
    extra, rest = rest[:n_extra], rest[n_extra:]
    cast_in, rest = rest[:n_cast], rest[n_cast:]
    outs, rest = rest[:n_out], rest[n_out:]
    cast_out, rest = rest[:n_cast], rest[n_cast:]
    xbuf, h_scr, sem = rest[:3]
    scratch = rest[3:]

    m, n = pl.program_id(0), pl.program_id(1)
    bm = xbuf.shape[0]

    def load(tile):
        return pltpu.make_async_copy(x_hbm.at[pl.ds(tile * bm, bm), :], xbuf, sem.at[0])

    @pl.when(jnp.logical_and(m == 0, n == 0))
    def _():
        load(0).start()

    @pl.when(n == 0)
    def _():
        load(m).wait()
        _normmod_rows(xbuf, g_ref, sh_ref, sc_ref, h_scr)

    @pl.when(jnp.logical_and(n == PROJ_PREFETCH_STEP, m + 1 < pl.num_programs(0)))
    def _():
        load(m + 1).start()

    for src, dst in zip(cast_in, cast_out):
        dst[...] = src[0].astype(dst.dtype)

    h = h_scr[...]
    parts = tuple(jnp.dot(h, w_ref[...], preferred_element_type=F32) for w_ref in w_refs)
    epilogue(parts, extra, outs, scratch)


def _proj_call(epilogue, name, x, g, shift, scale, w, extra_in, extra_specs, out_dtypes, scratch,
               side_casts, *, tiles_per_batch):
    m, d = x.shape
    n = w.shape[1] // 3
    bm = m // (shift.shape[0] * tiles_per_batch)
    bn = _tile(n, PROJ_COLS)
    nb = n // bn
    assert nb > PROJ_PREFETCH_STEP
    steps = (m // bm) * nb

    def wspec(group):
        return pl.BlockSpec((d, bn), lambda i, j: (0, j + group * nb))

    cast_in_specs, cast_out_specs, cast_shapes = [], [], []
    for wsrc, layer in side_casts:
        _, k, cols = wsrc.shape
        rows = k // steps
        assert rows * steps == k and rows % V7X_BF16_SUBLANES == 0
        cast_in_specs.append(pl.BlockSpec((1, rows, cols), lambda i, j, layer=layer: (layer, i * nb + j, 0)))
        cast_out_specs.append(pl.BlockSpec((rows, cols), lambda i, j: (i * nb + j, 0)))
        cast_shapes.append(jax.ShapeDtypeStruct((k, cols), BF16))

    per_batch = _per_batch_spec(d, tiles_per_batch)
    out = pl.BlockSpec((bm, bn), lambda i, j: (i, j))
    kernel_fn = functools.partial(
        _proj_kernel, epilogue=functools.partial(epilogue, tiles_per_batch=tiles_per_batch),
        n_extra=len(extra_in), n_out=len(out_dtypes), n_cast=len(side_casts))
    return pl.pallas_call(
        kernel_fn,
        grid=(m // bm, nb),
        in_specs=[pl.BlockSpec(memory_space=pl.ANY),
                  pl.BlockSpec((1, d), lambda i, j: (0, 0)), per_batch, per_batch,
                  wspec(0), wspec(1), wspec(2)] + extra_specs(bn) + cast_in_specs,
        out_specs=[out] * len(out_dtypes) + cast_out_specs,
        out_shape=[jax.ShapeDtypeStruct((m, n), dt) for dt in out_dtypes] + cast_shapes,
        scratch_shapes=[pltpu.VMEM((bm, d), F32), pltpu.VMEM((bm, d), BF16), pltpu.SemaphoreType.DMA((1,))]
        + scratch(bm, bn, nb),
        compiler_params=_params("arbitrary", "arbitrary"),
        name=name,
    )(x, g.reshape(1, d), shift, scale, w, w, w, *extra_in, *[wsrc for wsrc, _ in side_casts])


def _proj_glu(x, g, shift, scale, w, side_casts, *, tiles_per_batch):
    return _proj_call(_glu_epilogue, "proj_glu", x, g, shift, scale, w, [], lambda bn: [], [F32, F32],
                      lambda bm, bn, nb: [], side_casts, tiles_per_batch=tiles_per_batch)


def _proj_gate(x, g, shift, scale, w, conv_w, side_casts, *, tiles_per_batch):
    taps = conv_w.shape[0]
    assert taps - 1 <= SHORT_HALO_ROWS
    return _proj_call(
        _gate_epilogue, "proj_gate", x, g, shift, scale, w, [conv_w],
        lambda bn: [pl.BlockSpec((taps, bn), lambda i, j: (0, j))], [BF16],
        lambda bm, bn, nb: [pltpu.VMEM((SHORT_HALO_ROWS + bm, bn), F32),
                            pltpu.VMEM((nb, SHORT_HALO_ROWS, bn), F32)],
        side_casts, tiles_per_batch=tiles_per_batch)


def _mm_res_kernel(u_ref, w_ref, x_ref, gate_ref, o_ref):
    y = jnp.dot(u_ref[...], w_ref[...], preferred_element_type=F32)
    o_ref[...] = x_ref[...] + gate_ref[0] * y


def _matmul_residual(u, w, x, gate, *, tiles_per_batch):
    m, k = u.shape
    n = w.shape[1]
    bm = m // (gate.shape[0] * tiles_per_batch)
    bn = _tile(n, 1024)
    tile = pl.BlockSpec((bm, bn), lambda i, j: (i, j))
    return pl.pallas_call(
        _mm_res_kernel,
        grid=(m // bm, n // bn),
        in_specs=[pl.BlockSpec((bm, k), lambda i, j: (i, 0)),
                  pl.BlockSpec((k, bn), lambda i, j: (0, j)),
                  tile,
                  pl.BlockSpec((1, 1, bn), lambda i, j: (i // tiles_per_batch, 0, j))],
        out_specs=tile,
        out_shape=jax.ShapeDtypeStruct((m, n), F32),
        compiler_params=_params("parallel", "arbitrary"),
        name="matmul_residual",
    )(u, w, x, gate)


def _mixer_ab_kernel(a_ref, v_ref, ha_ref, hv_ref, cw_ref, cb_ref, lg_ref, lb_ref, pw_ref, ps_ref,
                     o_ref, abuf, shifted, ybuf, vbuf, pbuf):
    s = pl.program_id(1)
    ts = a_ref.shape[1]
    dc = a_ref.shape[2]
    dp = v_ref.shape[2]
    taps = cw_ref.shape[0]
    n_groups = pw_ref.shape[0]
    dg = dp // n_groups
    first = s == 0

    abuf[0:HALO_ROWS, 0:dc] = jnp.where(first, 0.0, ha_ref[0])
    abuf[HALO_ROWS:, 0:dc] = a_ref[0]
    vbuf[0:HALO_ROWS, :] = jnp.where(first, 0.0, hv_ref[0])
    vbuf[HALO_ROWS:, :] = v_ref[0]

    base = HALO_ROWS - (taps - 1)
    rows = V7X_SUBLANES
    n_shift = shifted.shape[1]
    for r in range(1, rows):
        shifted[r - 1, :, 0:dc] = abuf[pl.ds(r, n_shift), 0:dc]

    def tap(k, r0, lanes):
        q, r = divmod(base + k, rows)
        start = pl.multiple_of(r0 + q * rows, rows)
        if r == 0:
            return abuf[pl.ds(start, rows), lanes]
        return shifted[r - 1, pl.ds(start, rows), lanes]

    for c in range(dc // V7X_LANES):
        lanes = pl.ds(c * V7X_LANES, V7X_LANES)
        wk = [jnp.broadcast_to(cw_ref[k:k + 1, lanes], (rows, V7X_LANES)) for k in range(taps)]
        bias = jnp.broadcast_to(cb_ref[:, lanes], (rows, V7X_LANES))

        def row_body(i, carry, lanes=lanes, wk=wk, bias=bias):
            r0 = pl.multiple_of(i * rows, rows)
            parts = [wk[k] * tap(k, r0, lanes) for k in range(CONV_PARTIAL_SUMS)]
            for k in range(CONV_PARTIAL_SUMS, taps):
                parts[k % CONV_PARTIAL_SUMS] = parts[k % CONV_PARTIAL_SUMS] + wk[k] * tap(k, r0, lanes)
            while len(parts) > 1:
                parts = [p + q for p, q in zip(parts[0::2], parts[1::2])] + parts[len(parts) & ~1:]
            ybuf[pl.ds(r0, rows), lanes] = parts[0] + bias
            return carry

        lax.fori_loop(0, ts // rows, row_body, 0, unroll=4)

    y = ybuf[...]
    mu = jnp.mean(y, axis=-1, keepdims=True)
    yc = y - mu
    var = jnp.mean(yc * yc, axis=-1, keepdims=True)
    a_out = _silu(yc * lax.rsqrt(var + EPS) * lg_ref[...] + lb_ref[...])
    o_ref[0, :, 0:dc] = a_out.astype(o_ref.dtype)

    t1 = (s * ts + 1 + lax.broadcasted_iota(jnp.int32, (ts, dg), 0)).astype(F32)
    for g, w in enumerate(POOL_WINDOWS[:n_groups]):
        lanes = pl.ds(g * dg, dg)
        v = vbuf[pl.ds(HALO_ROWS, ts), lanes]
        src, src_lanes, d, stage = vbuf, lanes, 1, 0
        while 2 * d < w:
            lo = (stage + 1) * V7X_SUBLANES
            n = HALO_ROWS + ts - lo
            dst = pbuf.at[stage % 2]
            dst[pl.ds(lo, n), :] = src[pl.ds(lo, n), src_lanes] + src[pl.ds(lo - d, n), src_lanes]
            src, src_lanes, d, stage = dst, slice(None), 2 * d, stage + 1
        acc = src[pl.ds(HALO_ROWS, ts), src_lanes] + src[pl.ds(HALO_ROWS - d, ts), src_lanes]
        pooled = acc / jnp.minimum(t1, float(w)) - v
        mixed = jnp.dot(pooled.astype(BF16), pw_ref[g], preferred_element_type=F32)
        o_ref[0, :, pl.ds(dc + g * dg, dg)] = (mixed * ps_ref[:, lanes]).astype(o_ref.dtype)


def _mixer_ab(a, v, conv_w, conv_b, ln_g, ln_b, pool_w, pool_scale):
    b, s, dc = a.shape
    dp = v.shape[2]
    taps = conv_w.shape[0]
    assert taps - 1 <= HALO_ROWS
    assert all(w & (w - 1) == 0 for w in POOL_WINDOWS) and 2 * max(POOL_WINDOWS) <= HALO_ROWS
    ts = _tile(s, 256)
    halo_per_tile = ts // HALO_ROWS

    def cur(width):
        return pl.BlockSpec((1, ts, width), lambda i, j: (i, j, 0))

    def halo(width):
        return pl.BlockSpec((1, HALO_ROWS, width),
                            lambda i, j: (i, jnp.maximum(j * halo_per_tile - 1, 0), 0))

    def whole(shape):
        return pl.BlockSpec(shape, lambda i, j: (0,) * len(shape))

    return pl.pallas_call(
        _mixer_ab_kernel,
        grid=(b, s // ts),
        in_specs=[cur(dc), cur(dp), halo(dc), halo(dp),
                  whole((taps, dc)), whole((1, dc)), whole((1, dc)), whole((1, dc)),
                  whole(pool_w.shape), whole((1, dp))],
        out_specs=pl.BlockSpec((1, ts, dc + dp), lambda i, j: (i, j, 0)),
        out_shape=jax.ShapeDtypeStruct((b, s, dc + dp), BF16),
        scratch_shapes=[pltpu.VMEM((HALO_ROWS + ts, dc + V7X_LANES), F32),
                        pltpu.VMEM((V7X_SUBLANES - 1, HALO_ROWS + ts - V7X_SUBLANES, dc + V7X_LANES), F32),
                        pltpu.VMEM((ts, dc), F32),
                        pltpu.VMEM((HALO_ROWS + ts, dp), F32),
                        pltpu.VMEM((2, HALO_ROWS + ts, dp // pool_w.shape[0]), F32)],
        compiler_params=_params("parallel", "parallel"),
        name="mixer_ab",
    )(a, v, a, v,
      conv_w, conv_b.reshape(1, dc), ln_g.reshape(1, dc), ln_b.reshape(1, dc),
      pool_w.astype(BF16), pool_scale.reshape(1, dp))


FFN_PREFETCH_STEP = 1


def _ffn_kernel(x_hbm, g_ref, sh_ref, sc_ref, gate_ref, fg_ref, w1_ref, w3_ref, w2_ref, o_hbm,
                acc, h_scr, in_sem, out_sem, *, final_norm):
    m, f = pl.program_id(0), pl.program_id(1)
    n_m, n_f = pl.num_programs(0), pl.num_programs(1)
    bm = acc.shape[1]
    slot = m % 2
    other = 1 - slot

    def load(tile, s):
        return pltpu.make_async_copy(x_hbm.at[pl.ds(tile * bm, bm), :], acc.at[s], in_sem.at[s])

    def store(tile, s):
        return pltpu.make_async_copy(acc.at[s], o_hbm.at[pl.ds(tile * bm, bm), :], out_sem.at[s])

    @pl.when(jnp.logical_and(m == 0, f == 0))
    def _():
        load(0, 0).start()

    @pl.when(f == 0)
    def _():
        load(m, slot).wait()
        _normmod_rows(acc.at[slot], g_ref, sh_ref, sc_ref, h_scr)

    @pl.when(f == FFN_PREFETCH_STEP)
    def _():
        @pl.when(m >= 1)
        def _():
            store(m - 1, other).wait()

        @pl.when(m + 1 < n_m)
        def _():
            load(m + 1, other).start()

    h = h_scr[...]
    a = jnp.dot(h, w1_ref[...], preferred_element_type=F32)
    b = jnp.dot(h, w3_ref[...], preferred_element_type=F32)
    u = (_silu(a) * b).astype(BF16)
    acc[slot] += gate_ref[0] * jnp.dot(u, w2_ref[...], preferred_element_type=F32)

    @pl.when(f == n_f - 1)
    def _():
        if final_norm:
            fg = fg_ref[...]
            chunk = _tile(bm, NORM_CHUNK)

            def body(i, carry):
                rows = pl.ds(pl.multiple_of(i * chunk, chunk), chunk)
                acc[slot, rows, :] = _rms(acc[slot, rows, :], fg)
                return carry

            lax.fori_loop(0, bm // chunk, body, 0)
        store(m, slot).start()

        @pl.when(m == n_m - 1)
        def _():
            store(m, slot).wait()


def _ffn(x, g, shift, scale, gate, final_g, w1, w3, w2, *, tiles_per_batch, final_norm):
    m, d = x.shape
    dff = w1.shape[1]
    bm = m // (gate.shape[0] * tiles_per_batch)
    bf = _tile(dff, 256)
    assert dff // bf > FFN_PREFETCH_STEP
    per_batch = _per_batch_spec(d, tiles_per_batch)
    shared = pl.BlockSpec((1, d), lambda i, j: (0, 0))
    return pl.pallas_call(
        functools.partial(_ffn_kernel, final_norm=final_norm),
        grid=(m // bm, dff // bf),
        in_specs=[pl.BlockSpec(memory_space=pl.ANY),
                  shared, per_batch, per_batch, per_batch, shared,
                  pl.BlockSpec((d, bf), lambda i, j: (0, j)),
                  pl.BlockSpec((d, bf), lambda i, j: (0, j)),
                  pl.BlockSpec((bf, d), lambda i, j: (j, 0))],
        out_specs=pl.BlockSpec(memory_space=pl.ANY),
        out_shape=jax.ShapeDtypeStruct((m, d), F32),
        scratch_shapes=[pltpu.VMEM((2, bm, d), F32), pltpu.VMEM((bm, d), BF16),
                        pltpu.SemaphoreType.DMA((2,)), pltpu.SemaphoreType.DMA((2,))],
        compiler_params=_params("arbitrary", "arbitrary"),
        name="ffn_final" if final_norm else "ffn",
    )(x, g.reshape(1, d), shift, scale, gate, final_g.reshape(1, d), w1, w3, w2)


def kernel(x, c, mod_w, mod_b, norm_mix_g, norm_ffn_g, ab_in_w, conv_a_w, conv_a_b, ln_a_g, ln_a_b,
           pool_w, pool_scale, ab_out_w, c_in_w, conv_c_w, c_out_w, ffn_w1, ffn_w3, ffn_w2, final_g):
    b, s, d = x.shape
    depth = mod_w.shape[0]
    m = b * s
    tiles_per_batch = s // _tile(s, ROW_TILE)
    mod = _modulation(c, mod_w, mod_b)

    def mod_part(i, part):
        return mod[i, :, part * d:(part + 1) * d].reshape(b, 1, d)

    x = x.reshape(m, d)
    for i in range(depth):
        j = i // 2
        sh_m, sc_m, g_m, sh_f, sc_f, g_f = (mod_part(i, p) for p in range(6))
        ffn_casts = [(ffn_w1, i), (ffn_w3, i)]
        if i % 2 == 0:
            a, v, w1, w3 = _proj_glu(x, norm_mix_g[i], sh_m, sc_m, _to_bf16(ab_in_w, j), ffn_casts,
                                     tiles_per_batch=tiles_per_batch)
            u = _mixer_ab(a.reshape(b, s, -1), v.reshape(b, s, -1), conv_a_w[j], conv_a_b[j],
                          ln_a_g[j], ln_a_b[j], pool_w[j], pool_scale[j])
            x = _matmul_residual(u.reshape(m, -1), _to_bf16(ab_out_w, j), x, g_m,
                                 tiles_per_batch=tiles_per_batch)
        else:
            u, w1, w3 = _proj_gate(x, norm_mix_g[i], sh_m, sc_m, _to_bf16(c_in_w, j), conv_c_w[j],
                                   ffn_casts, tiles_per_batch=tiles_per_batch)
            x = _matmul_residual(u, _to_bf16(c_out_w, j), x, g_m, tiles_per_batch=tiles_per_batch)
        x = _ffn(x, norm_ffn_g[i], sh_f, sc_f, g_f, final_g, w1, w3, _to_bf16(ffn_w2, i),
                 tiles_per_batch=tiles_per_batch, final_norm=(i + 1 == depth))
    return x.reshape(b, s, d)
```
